```python
import math
import jax, jax.numpy as jnp
from jax import lax
import numpy as np

D_MODEL = 2048
BATCH = 1
SEQ = 8192
DEPTH = 2

HEAD_DIM = 64
BLOCK = 128
GROUP_WIDTH = D_MODEL // 4
MIX_WIDTH = 4 * GROUP_WIDTH
ROPE_THETA = 10000.0
NEG_INF = -1e30
LN_EPS = 1e-5
RMS_EPS = 1e-5

A_HEADS = GROUP_WIDTH // HEAD_DIM
A_KV_HEADS = 2
SWA_WINDOW = 128
B_HEADS = GROUP_WIDTH // (2 * HEAD_DIM)
B_VDIM = 2 * HEAD_DIM
C_HEADS = GROUP_WIDTH // HEAD_DIM
DILATION_PAIRS = ((128, 1), (512, 4), (2048, 16))
D_HEADS = GROUP_WIDTH // HEAD_DIM
IDX_HEADS = 8
IDX_DIM = 64
DSA_TOPK_MAX = 256
PEER_HEADS = 8
PEER_TOPK = 16
N_KEYS = 128
N_EXPERTS = N_KEYS * N_KEYS
PEER_QDIM = 256
PEER_HALF = PEER_QDIM // 2
PEER_BLOCK = 128
ALPHA = (2 * DEPTH) ** 0.25
BETA = (8 * DEPTH) ** -0.25

IN_SPLITS = (
    A_HEADS * HEAD_DIM, A_KV_HEADS * HEAD_DIM, A_KV_HEADS * HEAD_DIM,
    B_HEADS * 2 * HEAD_DIM, B_HEADS * 2 * HEAD_DIM, B_HEADS * B_VDIM,
    C_HEADS * HEAD_DIM, C_HEADS * HEAD_DIM, C_HEADS * HEAD_DIM,
    D_HEADS * HEAD_DIM, HEAD_DIM, HEAD_DIM,
    IDX_HEADS * IDX_DIM, IDX_DIM, IDX_HEADS,
)
VALUE_SLOTS = (2, 5, 8, 11)
IN_WIDTH = sum(IN_SPLITS)

kernel_name = 'hybrid_headgroup_swa_diff_dilated_dsa_peer_deepnorm'


def layer_norm(x, g, b):
    xf = x.astype(jnp.float32)
    mu = jnp.mean(xf, -1, keepdims=True)
    var = jnp.mean(jnp.square(xf - mu), -1, keepdims=True)
    y = (xf - mu) * lax.rsqrt(var + LN_EPS)
    return (y * g.astype(jnp.float32) + b.astype(jnp.float32)).astype(x.dtype)


def rope_tables(seq_len):
    half = HEAD_DIM // 2
    inv_freq = ROPE_THETA ** (-jnp.arange(half, dtype=jnp.float32) / half)
    ang = jnp.arange(seq_len, dtype=jnp.float32)[:, None] * inv_freq[None, :]
    return jnp.cos(ang), jnp.sin(ang)


def apply_rope(t, cos, sin):
    half = t.shape[-1] // 2
    tf = t.astype(jnp.float32)
    t1, t2 = tf[..., :half], tf[..., half:]
    return jnp.concatenate([t1 * cos - t2 * sin, t1 * sin + t2 * cos], -1).astype(t.dtype)


def banded_attention(q, k, v, max_dist):
    b, hk, g, L, hd = q.shape
    nb = -(-L // BLOCK)
    pad = nb * BLOCK - L
    qp = jnp.pad(q, ((0, 0), (0, 0), (0, 0), (0, pad), (0, 0))).reshape(b, hk, g, nb, BLOCK, hd)

    def key_blocks(t):
        tp = jnp.pad(t, ((0, 0), (0, 0), (BLOCK, pad), (0, 0))).reshape(b, hk, nb + 1, BLOCK, t.shape[-1])
        return jnp.concatenate([tp[:, :, :-1], tp[:, :, 1:]], axis=3)

    kb, vb = key_blocks(k), key_blocks(v)
    s = jnp.einsum('bhgnqd,bhnkd->bhgnqk', qp.astype(jnp.float32), kb.astype(jnp.float32)) * HEAD_DIM ** -0.5
    qi = jnp.arange(BLOCK)[:, None] + BLOCK
    ki = jnp.arange(2 * BLOCK)[None, :]
    dist = qi - ki
    kabs = jnp.arange(nb)[:, None, None] * BLOCK - BLOCK + ki[None]
    mask = ((dist >= 0) & (dist <= max_dist))[None] & (kabs >= 0)
    s = jnp.where(mask, s, NEG_INF)
    m = jnp.max(s, -1)
    p = jnp.exp(s - m[..., None])
    l = jnp.sum(p, -1)
    acc = jnp.einsum('bhgnqk,bhnkd->bhgnqd', p, vb.astype(jnp.float32))
    acc = acc.reshape(b, hk, g, nb * BLOCK, -1)[..., :L, :]
    m = m.reshape(b, hk, g, nb * BLOCK)[..., :L]
    l = l.reshape(b, hk, g, nb * BLOCK)[..., :L]
    return acc, m, l


def fold_residues(t, dil):
    bsz, h, L = t.shape[:3]
    rest = t.shape[3:]
    t = t.reshape(bsz, h, L // dil, dil, *rest)
    t = jnp.moveaxis(t, 3, 1)
    return t.reshape(bsz * dil, h, L // dil, *rest)


def unfold_residues(t, batch, dil):
    bd, h, ld = t.shape[:3]
    rest = t.shape[3:]
    t = t.reshape(batch, dil, h, ld, *rest)
    t = jnp.moveaxis(t, 1, 3)
    return t.reshape(batch, h, ld * dil, *rest)


def dilated_mixture(q, k, v):
    bsz = q.shape[0]
    accs, ms, ls = [], [], []
    for window, dil in DILATION_PAIRS:
        acc, m, l = banded_attention(fold_residues(q, dil)[:, :, None], fold_residues(k, dil),
                                     fold_residues(v, dil), window // dil)
        accs.append(unfold_residues(acc[:, :, 0], bsz, dil))
        ms.append(unfold_residues(m[:, :, 0], bsz, dil))
        ls.append(unfold_residues(l[:, :, 0], bsz, dil))
    m_all = jnp.stack(ms)
    w = jnp.exp(m_all - jnp.max(m_all, 0))
    num = jnp.sum(w[..., None] * jnp.stack(accs), 0)
    den = jnp.sum(w * jnp.stack(ls), 0)
    return num / den[..., None]


def diff_attention(q1, q2, k1, k2, v, lam):
    bsz, h, L, hd = q1.shape
    nb = L // BLOCK
    kpos = jnp.arange(L)
    k1f, k2f, vf = k1.astype(jnp.float32), k2.astype(jnp.float32), v.astype(jnp.float32)
    scale = HEAD_DIM ** -0.5

    def blk(inp):
        qb1, qb2, n = inp
        qpos = n * BLOCK + jnp.arange(BLOCK)
        causal = kpos[None, :] <= qpos[:, None]
        s1 = jnp.where(causal, jnp.einsum('bhqd,bhkd->bhqk', qb1.astype(jnp.float32), k1f) * scale, NEG_INF)
        s2 = jnp.where(causal, jnp.einsum('bhqd,bhkd->bhqk', qb2.astype(jnp.float32), k2f) * scale, NEG_INF)
        a = jax.nn.softmax(s1, -1) - lam * jax.nn.softmax(s2, -1)
        return jnp.einsum('bhqk,bhkd->bhqd', a, vf)

    def to_blocks(t):
        return t.reshape(bsz, h, nb, BLOCK, t.shape[-1]).transpose(2, 0, 1, 3, 4)

    out = lax.map(blk, (to_blocks(q1), to_blocks(q2), jnp.arange(nb)))
    return out.transpose(1, 2, 0, 3, 4).reshape(bsz, h, L, -1)


def dsa_attention(q, k, v, iq, ik, iw, topk):
    bsz, h, L, hd = q.shape
    nb = L // BLOCK
    kpos = jnp.arange(L)
    ikf = ik.astype(jnp.float32)

    def blk(inp):
        qb, iqb, iwb, n = inp
        qpos = n * BLOCK + jnp.arange(BLOCK)
        sc = jax.nn.relu(jnp.einsum('bhqd,bsd->bqhs', iqb.astype(jnp.float32), ikf) * IDX_DIM ** -0.5)
        score = jnp.einsum('bqh,bqhs->bqs', iwb.astype(jnp.float32) * IDX_HEADS ** -0.5, sc)
        causal = kpos[None, :] <= qpos[:, None]
        score = jnp.where(causal[None], score, NEG_INF)
        _, idx = lax.top_k(score, topk)
        ksel = jax.vmap(lambda kk, ii: kk[ii])(k, idx).astype(jnp.float32)
        vsel = jax.vmap(lambda vv, ii: vv[ii])(v, idx).astype(jnp.float32)
        valid = idx <= qpos[None, :, None]
        s = jnp.einsum('bhqd,bqkd->bhqk', qb.astype(jnp.float32), ksel) * HEAD_DIM ** -0.5
        s = jnp.where(valid[:, None], s, NEG_INF)
        p = jax.nn.softmax(s, -1)
        return jnp.einsum('bhqk,bqkd->bhqd', p, vsel)

    def to_blocks(t):
        return t.reshape(bsz, t.shape[1], nb, BLOCK, t.shape[-1]).transpose(2, 0, 1, 3, 4)

    iwb = iw.reshape(bsz, nb, BLOCK, IDX_HEADS).transpose(1, 0, 2, 3)
    out = lax.map(blk, (to_blocks(q), to_blocks(iq), iwb, jnp.arange(nb)))
    return out.transpose(1, 2, 0, 3, 4).reshape(bsz, h, L, hd)


def hybrid_mixer(x, w_in, sinks, lam_p, sub_g, w_out, layer_idx, cos, sin):
    b, L, _ = x.shape
    proj = jnp.einsum('bld,dk->blk', x, w_in)
    points = [int(p) for p in np.cumsum(IN_SPLITS)[:-1]]
    (aq, ak, av, bq, bk, bv, cq, ck, cv, dq, dk, dv, iq, ik, iw) = jnp.split(proj, points, axis=-1)

    def heads(t, n):
        return t.reshape(b, L, n, -1).transpose(0, 2, 1, 3)

    def merge(o):
        return o.transpose(0, 2, 1, 3).reshape(b, L, -1)

    qa = apply_rope(heads(aq, A_HEADS), cos, sin).reshape(b, A_KV_HEADS, A_HEADS // A_KV_HEADS, L, HEAD_DIM)
    ka = apply_rope(heads(ak, A_KV_HEADS), cos, sin)
    va = heads(av, A_KV_HEADS)
    acc, m, l = banded_attention(qa, ka, va, SWA_WINDOW - 1)
    sink = sinks.astype(jnp.float32).reshape(1, A_KV_HEADS, -1, 1)
    m_all = jnp.maximum(m, sink)
    corr = jnp.exp(m - m_all)
    den = l * corr + jnp.exp(sink - m_all)
    oa = (acc * (corr / den)[..., None]).reshape(b, A_HEADS, L, HEAD_DIM)

    bq2 = bq.reshape(b, L, B_HEADS, 2, HEAD_DIM).transpose(3, 0, 2, 1, 4)
    bk2 = bk.reshape(b, L, B_HEADS, 2, HEAD_DIM).transpose(3, 0, 2, 1, 4)
    lam_init = 0.8 - 0.6 * math.exp(-0.3 * layer_idx)
    lp = lam_p.astype(jnp.float32)
    lam = jnp.exp(jnp.sum(lp[0] * lp[1])) - jnp.exp(jnp.sum(lp[2] * lp[3])) + lam_init
    ob = diff_attention(apply_rope(bq2[0], cos, sin), apply_rope(bq2[1], cos, sin),
                        apply_rope(bk2[0], cos, sin), apply_rope(bk2[1], cos, sin),
                        heads(bv, B_HEADS), lam)
    ob = ob * lax.rsqrt(jnp.mean(jnp.square(ob), -1, keepdims=True) + RMS_EPS) \
        * sub_g.astype(jnp.float32) * (1.0 - lam_init)

    oc = dilated_mixture(apply_rope(heads(cq, C_HEADS), cos, sin), apply_rope(heads(ck, C_HEADS), cos, sin),
                         heads(cv, C_HEADS))

    topk = min(DSA_TOPK_MAX, L // 4)
    od = dsa_attention(apply_rope(heads(dq, D_HEADS), cos, sin), apply_rope(dk, cos, sin), dv,
                       apply_rope(heads(iq, IDX_HEADS), cos, sin), apply_rope(ik, cos, sin), iw, topk)

    mixed = jnp.concatenate([merge(oa), merge(ob), merge(oc), merge(od)], -1).astype(x.dtype)
    return jnp.einsum('blk,kd->bld', mixed, w_out)


def peer_ffn(x, wq, sub_keys, u, v):
    b, L, d = x.shape
    q = jnp.einsum('bld,dk->blk', x, wq).reshape(b, L, PEER_HEADS, 2, PEER_HALF).astype(jnp.float32)
    s = jnp.einsum('blhpc,hpnc->blhpn', q, sub_keys.astype(jnp.float32))
    s1, i1 = lax.top_k(s[..., 0, :], PEER_TOPK)
    s2, i2 = lax.top_k(s[..., 1, :], PEER_TOPK)
    cand = (s1[..., :, None] + s2[..., None, :]).reshape(b, L, PEER_HEADS, PEER_TOPK * PEER_TOPK)
    cidx = (i1[..., :, None] * N_KEYS + i2[..., None, :]).reshape(b, L, PEER_HEADS, PEER_TOPK * PEER_TOPK)
    top_s, pos = lax.top_k(cand, PEER_TOPK)
    eidx = jnp.take_along_axis(cidx, pos, axis=-1)
    g = jax.nn.softmax(top_s, -1)
    n_sel = PEER_HEADS * PEER_TOPK
    nb = (b * L) // PEER_BLOCK
    xb = x.reshape(nb, PEER_BLOCK, d)
    eb = eidx.reshape(nb, PEER_BLOCK, n_sel)
    gb = g.reshape(nb, PEER_BLOCK, n_sel)

    def blk(inp):
        xt, et, gt = inp
        h = jnp.einsum('td,tkd->tk', xt, u[et]).astype(jnp.float32)
        a = gt * jax.nn.gelu(h, approximate=False)
        return jnp.einsum('tk,tkd->td', a.astype(v.dtype), v[et])

    out = lax.map(blk, (xb, eb, gb))
    return out.reshape(b, L, d).astype(x.dtype)


def setup_inputs(seed: int = 0) -> dict:
    key = jax.random.key(seed)
    ks = jax.random.split(key, 14)
    f = jnp.float32
    offsets = np.concatenate([[0], np.cumsum(IN_SPLITS)])
    col_scale = np.ones((IN_WIDTH,), np.float32)
    for slot in VALUE_SLOTS:
        col_scale[offsets[slot]:offsets[slot + 1]] = BETA
    x = jax.random.normal(ks[0], (BATCH, SEQ, D_MODEL), f)
    w_in = jax.random.normal(ks[1], (DEPTH, D_MODEL, IN_WIDTH), f) * (D_MODEL ** -0.5) * jnp.asarray(col_scale)
    attn_sinks = 0.5 * jax.random.normal(ks[2], (DEPTH, A_HEADS), f)
    diff_lambda = 0.1 * jax.random.normal(ks[3], (DEPTH, 4, HEAD_DIM), f)
    diff_norm_g = 1.0 + 0.02 * jax.random.normal(ks[4], (DEPTH, B_VDIM), f)
    w_out = jax.random.normal(ks[5], (DEPTH, MIX_WIDTH, D_MODEL), f) * (MIX_WIDTH ** -0.5) * BETA
    ln1_g = 1.0 + 0.02 * jax.random.normal(ks[6], (DEPTH, D_MODEL), f)
    ln1_b = 0.02 * jax.random.normal(ks[7], (DEPTH, D_MODEL), f)
    peer_wq = jax.random.normal(ks[8], (DEPTH, D_MODEL, PEER_HEADS * PEER_QDIM), f) * (D_MODEL ** -0.5)
    peer_keys = jax.random.normal(ks[9], (DEPTH, PEER_HEADS, 2, N_KEYS, PEER_HALF), f) * (PEER_HALF ** -0.5)
    peer_u = jax.random.normal(ks[10], (DEPTH, N_EXPERTS, D_MODEL), f) * (D_MODEL ** -0.5)
    peer_v = jax.random.normal(ks[11], (DEPTH, N_EXPERTS, D_MODEL), f) * BETA * (PEER_HEADS ** -0.5)
    ln2_g = 1.0 + 0.02 * jax.random.normal(ks[12], (DEPTH, D_MODEL), f)
    ln2_b = 0.02 * jax.random.normal(ks[13], (DEPTH, D_MODEL), f)
    return {'x': x, 'w_in': w_in, 'attn_sinks': attn_sinks, 'diff_lambda': diff_lambda,
            'diff_norm_g': diff_norm_g, 'w_out': w_out, 'ln1_g': ln1_g, 'ln1_b': ln1_b,
            'peer_wq': peer_wq, 'peer_keys': peer_keys, 'peer_u': peer_u, 'peer_v': peer_v,
            'ln2_g': ln2_g, 'ln2_b': ln2_b}


def reference(x, w_in, attn_sinks, diff_lambda, diff_norm_g, w_out, ln1_g, ln1_b,
              peer_wq, peer_keys, peer_u, peer_v, ln2_g, ln2_b):
    L = x.shape[1]
    cos, sin = rope_tables(L)
    for i in range(DEPTH):
        y = hybrid_mixer(x, w_in[i], attn_sinks[i], diff_lambda[i], diff_norm_g[i], w_out[i], i, cos, sin)
        x = layer_norm(ALPHA * x + y, ln1_g[i], ln1_b[i])
        y = peer_ffn(x, peer_wq[i], peer_keys[i], peer_u[i], peer_v[i])
        x = layer_norm(ALPHA * x + y, ln2_g[i], ln2_b[i])
    return x
```

```python
import functools
import math

import jax
import jax.numpy as jnp
import numpy as np
from jax import lax
from jax.experimental import pallas as pl
from jax.experimental.pallas import tpu as pltpu

F32 = jnp.float32
BF16 = jnp.bfloat16
I32 = jnp.int32

D_MODEL = 2048
DEPTH = 2
HEAD_DIM = 64
HALF_DIM = HEAD_DIM // 2
LANES = 128
BLOCK = 128
NEG_INF = -1e30
ROPE_THETA = 10000.0
LN_EPS = 1e-5
RMS_EPS = 1e-5
A_HEADS, A_KV_HEADS, SWA_WINDOW = 8, 2, 128
B_HEADS = 4
C_HEADS = 8
DILATION_PAIRS = ((128, 1), (512, 4), (2048, 16))
D_HEADS, IDX_HEADS, IDX_DIM = 8, 8, 64
DSA_TOPK_MAX = 256
PEER_HEADS, PEER_TOPK, N_KEYS = 8, 16, 128
PEER_QDIM = 256
PEER_HALF = PEER_QDIM // 2
ALPHA = (2 * DEPTH) ** 0.25
Q_SCALE = HEAD_DIM ** -0.5
INT_MIN = -2 ** 31

IN_SPLITS = (512, 128, 128, 512, 512, 512, 512, 512, 512, 512, 64, 64, 512, 64, 8)
IN_OFFSETS = tuple(int(v) for v in np.concatenate([[0], np.cumsum(IN_SPLITS)]))
(SL_AQ, SL_AK, SL_AV, SL_BQ, SL_BK, SL_BV, SL_CQ, SL_CK, SL_CV,
 SL_DQ, SL_DK, SL_DV, SL_IQ, SL_IK, SL_IW) = range(15)
ROPE_SLOTS = (SL_AQ, SL_BQ, SL_BK, SL_CQ, SL_CK, SL_DQ, SL_IQ, SL_AK, SL_DK, SL_IK)
ROPE_OFF = {}
_o = 0
for _s in ROPE_SLOTS:
    ROPE_OFF[_s] = _o
    _o += IN_SPLITS[_s]
ROPE_WIDTH = _o
VAL_OFF = {SL_BV: 0, SL_CV: 512, SL_AV: 1024, SL_DV: 1152}
VAL_WIDTH = 1280
VMEM_LIMIT = 56 * 1024 * 1024

NT_DIMS = (((1,), (1,)), ((), ()))
TN_DIMS = (((0,), (0,)), ((), ()))


def _cparams(*sem):
    return pltpu.CompilerParams(dimension_semantics=sem, vmem_limit_bytes=VMEM_LIMIT)


def _proj_kernel(x_ref, w_ref, *rest, rope):
    if rope:
        cos_ref, sa_ref, sb_ref, o_ref, xb_ref = rest
    else:
        o_ref, xb_ref = rest

    @pl.when(pl.program_id(1) == 0)
    def _():
        xb_ref[...] = x_ref[...].astype(BF16)

    y = jnp.dot(xb_ref[...], w_ref[...], preferred_element_type=F32)
    if rope:
        for t in range(y.shape[1] // LANES):
            sl = slice(t * LANES, (t + 1) * LANES)
            yt = y[:, sl]
            o_ref[:, sl] = (yt * cos_ref[:, sl]
                            + pltpu.roll(yt, LANES - HALF_DIM, 1) * sa_ref[:, sl]
                            + pltpu.roll(yt, HALF_DIM, 1) * sb_ref[:, sl]).astype(o_ref.dtype)
    else:
        o_ref[...] = y.astype(o_ref.dtype)


def _project(x, w, tables, out_dtype, tm, tn):
    L, K = x.shape
    N = w.shape[1]
    rope = tables is not None
    in_specs = [pl.BlockSpec((tm, K), lambda i, j: (i, 0)),
                pl.BlockSpec((K, tn), lambda i, j: (0, j))]
    args = [x, w]
    if rope:
        in_specs += [pl.BlockSpec((tm, tn), lambda i, j: (i, 0))] * 3
        args += list(tables)
    return pl.pallas_call(
        functools.partial(_proj_kernel, rope=rope),
        grid=(L // tm, N // tn),
        in_specs=in_specs,
        out_specs=pl.BlockSpec((tm, tn), lambda i, j: (i, j)),
        out_shape=jax.ShapeDtypeStruct((L, N), out_dtype),
        scratch_shapes=[pltpu.VMEM((tm, K), BF16)],
        compiler_params=_cparams("parallel", "arbitrary"),
        name="proj_rope" if rope else "proj_plain",
    )(*args)


def _rope_tables(L, width):
    inv_freq = ROPE_THETA ** (-jnp.arange(HALF_DIM, dtype=F32) / HALF_DIM)
    ang = jnp.arange(L, dtype=F32)[:, None] * inv_freq[None, :]
    cos, sin = jnp.cos(ang), jnp.sin(ang)
    zero = jnp.zeros_like(sin)
    reps = width // HEAD_DIM
    cos_t = jnp.tile(jnp.concatenate([cos, cos], 1), (1, reps))
    sin_a = jnp.tile(jnp.concatenate([-sin, zero], 1), (1, reps))
    sin_b = jnp.tile(jnp.concatenate([zero, sin], 1), (1, reps))
    return cos_t, sin_a, sin_b


def _banded_kernel(*refs, n_kv, group, max_dist, use_sink):
    if use_sink:
        sink_ref, q_ref, kp_ref, kc_ref, vp_ref, vc_ref, o_ref = refs
    else:
        q_ref, kp_ref, kc_ref, vp_ref, vc_ref, o_ref, lse_ref = refs
    n = pl.program_id(1)
    q = q_ref[...] * Q_SCALE
    k2 = jnp.concatenate([kp_ref[...], kc_ref[...]], 0)
    v2 = jnp.concatenate([vp_ref[...], vc_ref[...]], 0)
    row = lax.broadcasted_iota(I32, (BLOCK, 2 * BLOCK), 0)
    col = lax.broadcasted_iota(I32, (BLOCK, 2 * BLOCK), 1)
    dist = row + BLOCK - col
    lo = jnp.where(n > 0, 0, BLOCK)
    ok = (dist >= 0) & (dist <= max_dist) & (col >= lo)
    lane = lax.broadcasted_iota(I32, (BLOCK, LANES), 1)
    outs = []
    lse_tile = jnp.zeros((BLOCK, LANES), F32)
    for g in range(n_kv):
        kh = k2[:, g * HEAD_DIM:(g + 1) * HEAD_DIM]
        vh = v2[:, g * HEAD_DIM:(g + 1) * HEAD_DIM]
        for u in range(group):
            h = g * group + u
            qh = q[:, h * HEAD_DIM:(h + 1) * HEAD_DIM]
            s = lax.dot_general(qh, kh, NT_DIMS, preferred_element_type=F32)
            s = jnp.where(ok, s, NEG_INF)
            m = jnp.max(s, axis=1, keepdims=True)
            if use_sink:
                m = jnp.maximum(m, sink_ref[h])
            p = jnp.exp(s - m)
            l = jnp.sum(p, axis=1, keepdims=True)
            if use_sink:
                l = l + jnp.exp(sink_ref[h] - m)
            acc = jnp.dot(p.astype(BF16), vh, preferred_element_type=F32)
            outs.append(acc / l)
            if not use_sink:
                lse_tile = jnp.where(lane == h, m + jnp.log(l), lse_tile)
    o_ref[...] = jnp.concatenate(outs, 1).astype(o_ref.dtype)
    if not use_sink:
        lse_ref[...] = lse_tile


def _banded(q3, qcol, k3, kcol, v3, vcol, n_heads, n_kv, max_dist, sinks=None):
    d, Lf, _ = q3.shape
    nb = Lf // BLOCK
    wq, wk = n_heads * HEAD_DIM, n_kv * HEAD_DIM
    use_sink = sinks is not None
    cur = lambda c: (lambda r, n: (r, n, c))
    prev = lambda c: (lambda r, n: (r, jnp.maximum(n - 1, 0), c))
    in_specs = [pl.BlockSpec((None, BLOCK, wq), cur(qcol)),
                pl.BlockSpec((None, BLOCK, wk), prev(kcol)),
                pl.BlockSpec((None, BLOCK, wk), cur(kcol)),
                pl.BlockSpec((None, BLOCK, wk), prev(vcol)),
                pl.BlockSpec((None, BLOCK, wk), cur(vcol))]
    args = [q3, k3, k3, v3, v3]
    if use_sink:
        in_specs = [pl.BlockSpec(memory_space=pltpu.SMEM)] + in_specs
        args = [sinks] + args
        out_shape = jax.ShapeDtypeStruct((d, Lf, wq), BF16)
        out_specs = pl.BlockSpec((None, BLOCK, wq), cur(0))
    else:
        out_shape = (jax.ShapeDtypeStruct((d, Lf, wq), F32),
                     jax.ShapeDtypeStruct((d, Lf, LANES), F32))
        out_specs = (pl.BlockSpec((None, BLOCK, wq), cur(0)),
                     pl.BlockSpec((None, BLOCK, LANES), cur(0)))
    return pl.pallas_call(
        functools.partial(_banded_kernel, n_kv=n_kv, group=n_heads // n_kv,
                          max_dist=max_dist, use_sink=use_sink),
        grid=(d, nb),
        in_specs=in_specs,
        out_specs=out_specs,
        out_shape=out_shape,
        compiler_params=_cparams("parallel", "parallel"),
        name="swa_sink" if use_sink else "dilated_band",
    )(*args)


def _diff_kernel(lam_ref, g_ref, q_ref, k_ref, v_ref, o_ref, m_ref, l_ref, acc_ref, *, tq,
                 lam_init):
    i = pl.program_id(1)
    lane = lax.broadcasted_iota(I32, (tq, LANES), 1)
    q = q_ref[...] * Q_SCALE
    zero = jnp.zeros_like(q)
    qz = (jnp.where(lane < HEAD_DIM, q, zero), jnp.where(lane >= HEAD_DIM, q, zero))
    m_ref[...] = jnp.full(m_ref.shape, NEG_INF, F32)
    l_ref[...] = jnp.zeros(l_ref.shape, F32)
    acc_ref[...] = jnp.zeros(acc_ref.shape, F32)

    def step(j, masked):
        kt = k_ref[pl.ds(pl.multiple_of(j * tq, tq), tq), :]
        vt = v_ref[pl.ds(pl.multiple_of(j * tq, tq), tq), :]
        for c in range(2):
            s = lax.dot_general(qz[c], kt, NT_DIMS, preferred_element_type=F32)
            if masked:
                row = lax.broadcasted_iota(I32, (tq, tq), 0)
                col = lax.broadcasted_iota(I32, (tq, tq), 1)
                s = jnp.where(col <= row, s, NEG_INF)
            m_old = m_ref[c]
            m_new = jnp.maximum(m_old, jnp.max(s, axis=1, keepdims=True))
            a = jnp.exp(m_old - m_new)
            p = jnp.exp(s - m_new)
            l_ref[c] = a * l_ref[c] + jnp.sum(p, axis=1, keepdims=True)
            acc_ref[c] = a * acc_ref[c] + jnp.dot(p.astype(BF16), vt,
                                                  preferred_element_type=F32)
            m_ref[c] = m_new

    def body(j, carry):
        step(j, False)
        return carry

    lax.fori_loop(0, i, body, 0)
    step(i, True)

    lp = lam_ref[...]
    lam = (jnp.exp(jnp.sum(lp[0:1] * lp[1:2], axis=1, keepdims=True))
           - jnp.exp(jnp.sum(lp[2:3] * lp[3:4], axis=1, keepdims=True)) + lam_init)
    ob = acc_ref[0] / l_ref[0] - lam * (acc_ref[1] / l_ref[1])
    ob = ob * lax.rsqrt(jnp.mean(ob * ob, axis=1, keepdims=True) + RMS_EPS)
    o_ref[...] = (ob * g_ref[...] * (1.0 - lam_init)).astype(o_ref.dtype)


def _diff_attention(rp, vp, lam_p, sub_g, lam_init, tq):
    L = rp.shape[0]
    qcol = ROPE_OFF[SL_BQ] // LANES
    kcol = ROPE_OFF[SL_BK] // LANES
    vcol = VAL_OFF[SL_BV] // LANES
    return pl.pallas_call(
        functools.partial(_diff_kernel, tq=tq, lam_init=lam_init),
        grid=(B_HEADS, L // tq),
        in_specs=[pl.BlockSpec((4, HEAD_DIM), lambda h, i: (0, 0)),
                  pl.BlockSpec((1, LANES), lambda h, i: (0, 0)),
                  pl.BlockSpec((tq, LANES), lambda h, i: (i, qcol + h)),
                  pl.BlockSpec((L, LANES), lambda h, i: (0, kcol + h)),
                  pl.BlockSpec((L, LANES), lambda h, i: (0, vcol + h))],
        out_specs=pl.BlockSpec((tq, LANES), lambda h, i: (i, h)),
        out_shape=jax.ShapeDtypeStruct((L, B_HEADS * LANES), BF16),
        scratch_shapes=[pltpu.VMEM((2, tq, 1), F32), pltpu.VMEM((2, tq, 1), F32),
                        pltpu.VMEM((2, tq, LANES), F32)],
        compiler_params=_cparams("parallel", "arbitrary"),
        name="diff_attn",
    )(lam_p, sub_g.reshape(1, LANES), rp, rp, vp)


def _sortable(score):
    b = pltpu.bitcast(score, I32)
    return jnp.where(b >= 0, b, b ^ 0x7FFFFFFF)


def _dsa_kernel(q_ref, iq_ref, iw_ref, k_ref, ik_ref, v_ref, o_ref, key_ref, *, tk, topk):
    n = pl.program_id(0)
    n_chunks = (n * BLOCK + BLOCK + tk - 1) // tk
    qpos = n * BLOCK + lax.broadcasted_iota(I32, (BLOCK, tk), 0)
    col = lax.broadcasted_iota(I32, (BLOCK, tk), 1)

    def stack_heads(t):
        return jnp.concatenate([t[:, h * HEAD_DIM:(h + 1) * HEAD_DIM] for h in range(D_HEADS)], 0)

    iq8 = stack_heads(iq_ref[...] * (IDX_DIM ** -0.5))
    iw = iw_ref[...] * (IDX_HEADS ** -0.5)
    w8 = [iw[:, h:h + 1] for h in range(IDX_HEADS)]

    def score_chunk(c, carry):
        start = pl.multiple_of(c * tk, tk)
        sc = lax.dot_general(iq8, ik_ref[pl.ds(start, tk), :], NT_DIMS,
                             preferred_element_type=F32)
        score = jnp.zeros((BLOCK, tk), F32)
        for h in range(IDX_HEADS):
            score = score + w8[h] * jnp.maximum(sc[h * BLOCK:(h + 1) * BLOCK], 0.0)
        score = jnp.where(start + col <= qpos, score, NEG_INF)
        key_ref[c] = _sortable(score)
        return carry

    lax.fori_loop(0, n_chunks, score_chunk, 0)

    thr = jnp.full((BLOCK, 1), INT_MIN, I32)
    for bit in range(31, -1, -1):
        cand = thr + np.int32(INT_MIN if bit == 31 else 1 << bit)

        def count_chunk(c, cnt, cand=cand):
            hit = jnp.where(key_ref[c] >= cand, 1, 0)
            for u in range(tk // LANES):
                cnt = cnt + hit[:, u * LANES:(u + 1) * LANES]
            return cnt

        cnt = lax.fori_loop(0, n_chunks, count_chunk, jnp.zeros((BLOCK, LANES), I32))
        thr = jnp.where(jnp.sum(cnt, axis=1, keepdims=True) >= topk, cand, thr)

    q8 = stack_heads(q_ref[...] * Q_SCALE)
    rows = D_HEADS * BLOCK

    def attend_chunk(c, carry):
        m_old, l_old, acc = carry
        start = pl.multiple_of(c * tk, tk)
        keep = (key_ref[c] >= thr) & (start + col <= qpos)
        bias = jnp.where(keep, 0.0, NEG_INF)[None]
        keepf = jnp.where(keep, 1.0, 0.0)[None]
        s = lax.dot_general(q8, k_ref[pl.ds(start, tk), :], NT_DIMS, preferred_element_type=F32)
        s = s.reshape(D_HEADS, BLOCK, tk) + bias
        m_new = jnp.maximum(m_old, jnp.max(s, axis=2, keepdims=True))
        a = jnp.exp(m_old - m_new)
        p = jnp.exp(s - m_new) * keepf
        l_new = a * l_old + jnp.sum(p, axis=2, keepdims=True)
        pv = jnp.dot(p.reshape(rows, tk).astype(BF16), v_ref[pl.ds(start, tk), :],
                     preferred_element_type=F32)
        acc = a * acc + pv.reshape(D_HEADS, BLOCK, HEAD_DIM)
        return m_new, l_new, acc

    init = (jnp.full((D_HEADS, BLOCK, 1), NEG_INF, F32), jnp.zeros((D_HEADS, BLOCK, 1), F32),
            jnp.zeros((D_HEADS, BLOCK, HEAD_DIM), F32))
    _, l_fin, acc = lax.fori_loop(0, n_chunks, attend_chunk, init)
    out = (acc / l_fin).reshape(rows, HEAD_DIM)
    o_ref[...] = jnp.concatenate([out[h * BLOCK:(h + 1) * BLOCK] for h in range(D_HEADS)],
                                 1).astype(o_ref.dtype)


def _dsa_attention(rp, iw, dk, ik, dv, tk):
    L = rp.shape[0]
    topk = min(DSA_TOPK_MAX, L // 4)
    wide = D_HEADS * HEAD_DIM
    return pl.pallas_call(
        functools.partial(_dsa_kernel, tk=tk, topk=topk),
        grid=(L // BLOCK,),
        in_specs=[pl.BlockSpec((BLOCK, wide), lambda n: (n, ROPE_OFF[SL_DQ] // wide)),
                  pl.BlockSpec((BLOCK, wide), lambda n: (n, ROPE_OFF[SL_IQ] // wide)),
                  pl.BlockSpec((BLOCK, LANES), lambda n: (n, 0)),
                  pl.BlockSpec((L, HEAD_DIM), lambda n: (0, 0)),
                  pl.BlockSpec((L, HEAD_DIM), lambda n: (0, 0)),
                  pl.BlockSpec((L, HEAD_DIM), lambda n: (0, 0))],
        out_specs=pl.BlockSpec((BLOCK, wide), lambda n: (n, 0)),
        out_shape=jax.ShapeDtypeStruct((L, wide), BF16),
        scratch_shapes=[pltpu.VMEM((L // tk, BLOCK, tk), I32)],
        compiler_params=_cparams("parallel"),
        name="dsa_attn",
    )(rp, rp, iw, dk, ik, dv)


def _layer_norm(z, g, b):
    mu = jnp.mean(z, axis=1, keepdims=True)
    zc = z - mu
    var = jnp.mean(zc * zc, axis=1, keepdims=True)
    return zc * lax.rsqrt(var + LN_EPS) * g + b


def _outproj_kernel(x_ref, oa_ref, ob_ref, oc1_ref, oc2_ref, oc3_ref, ls1_ref, ls2_ref, ls3_ref,
                    od_ref, w_ref, g_ref, b_ref, y_ref, yb_ref):
    ocs = (oc1_ref[...], oc2_ref[...], oc3_ref[...])
    lses = (ls1_ref[...], ls2_ref[...], ls3_ref[...])
    top = jnp.maximum(jnp.maximum(lses[0], lses[1]), lses[2])
    wts = [jnp.exp(t - top) for t in lses]
    inv = 1.0 / (wts[0] + wts[1] + wts[2])
    merged = []
    for h in range(C_HEADS):
        sl = slice(h * HEAD_DIM, (h + 1) * HEAD_DIM)
        num = sum(wts[c][:, h:h + 1] * ocs[c][:, sl] for c in range(3))
        merged.append(num * inv[:, h:h + 1])
    oc = jnp.concatenate(merged, 1).astype(BF16)
    mixed = jnp.concatenate([oa_ref[...], ob_ref[...], oc, od_ref[...]], 1)
    y = jnp.dot(mixed, w_ref[...], preferred_element_type=F32)
    out = _layer_norm(ALPHA * x_ref[...] + y, g_ref[...], b_ref[...])
    y_ref[...] = out
    yb_ref[...] = out.astype(BF16)


def _out_projection(x, oa, ob, ocs, lses, od, w, g, b, tm):
    L, D = x.shape
    gw = D // 4
    row = lambda i: (i, 0)
    fixed = lambda i: (0, 0)
    return pl.pallas_call(
        _outproj_kernel,
        grid=(L // tm,),
        in_specs=[pl.BlockSpec((tm, D), row), pl.BlockSpec((tm, gw), row),
                  pl.BlockSpec((tm, gw), row)]
                 + [pl.BlockSpec((tm, gw), row)] * 3 + [pl.BlockSpec((tm, LANES), row)] * 3
                 + [pl.BlockSpec((tm, gw), row), pl.BlockSpec((D, D), fixed),
                    pl.BlockSpec((1, D), fixed), pl.BlockSpec((1, D), fixed)],
        out_specs=(pl.BlockSpec((tm, D), row), pl.BlockSpec((tm, D), row)),
        out_shape=(jax.ShapeDtypeStruct((L, D), F32), jax.ShapeDtypeStruct((L, D), BF16)),
        compiler_params=_cparams("parallel"),
        name="outproj_ln",
    )(x, oa, ob, *ocs, *lses, od, w, g.reshape(1, D), b.reshape(1, D))


def _top_values(t, k):
    vals = []
    for _ in range(k):
        mx = jnp.max(t, axis=0, keepdims=True)
        vals.append(mx)
        t = jnp.where(t == mx, -jnp.inf, t)
    return vals


def _route_kernel(xb_ref, wq_ref, keys_ref, s1_ref, c1_ref, s2_ref, e2_ref, tau_ref):
    q = jnp.dot(xb_ref[...], wq_ref[...], preferred_element_type=F32).astype(BF16)
    for h in range(PEER_HEADS):
        st = []
        for p in range(2):
            lo = (2 * h + p) * PEER_HALF
            st.append(lax.dot_general(keys_ref[2 * h + p], q[:, lo:lo + PEER_HALF], NT_DIMS,
                                      preferred_element_type=F32))
        v1 = _top_values(st[0], PEER_TOPK)
        v2 = jnp.concatenate(_top_values(st[1], PEER_TOPK), 0)
        cand = jnp.concatenate([v1[i] + v2 for i in range(PEER_TOPK)], 0)
        mu = _top_values(cand, PEER_TOPK)
        z = sum(jnp.exp(m - mu[0]) for m in mu)
        s1_ref[h] = st[0]
        c1_ref[h] = jnp.exp(st[0] - v1[0]) / z
        s2_ref[h] = st[1]
        e2_ref[h] = jnp.exp(st[1] - v2[0:1])
        tau_ref[h] = mu[PEER_TOPK - 1]


def _peer_route(xb, wq, keys, tm):
    L, D = xb.shape
    big = jax.ShapeDtypeStruct((PEER_HEADS, N_KEYS, L), F32)
    big_spec = pl.BlockSpec((PEER_HEADS, N_KEYS, tm), lambda i: (0, 0, i))
    return pl.pallas_call(
        _route_kernel,
        grid=(L // tm,),
        in_specs=[pl.BlockSpec((tm, D), lambda i: (i, 0)),
                  pl.BlockSpec((D, PEER_HEADS * PEER_QDIM), lambda i: (0, 0)),
                  pl.BlockSpec((2 * PEER_HEADS, N_KEYS, PEER_HALF), lambda i: (0, 0, 0))],
        out_specs=(big_spec, big_spec, big_spec, big_spec,
                   pl.BlockSpec((PEER_HEADS, 1, tm), lambda i: (0, 0, i))),
        out_shape=(big, big, big, big, jax.ShapeDtypeStruct((PEER_HEADS, 1, L), F32)),
        compiler_params=_cparams("parallel"),
        name="peer_route",
    )(xb, wq, keys)


def _peer_kernel(xb_ref, s1_ref, c1_ref, s2_ref, e2_ref, tau_ref, u_ref, v_ref, y_ref, a_ref, *,
                 rows_per_tile):
    j = pl.program_id(1)

    @pl.when(j == 0)
    def _():
        y_ref[...] = jnp.zeros(y_ref.shape, F32)

    xb = xb_ref[...]
    for r in range(rows_per_tile):
        a_idx = j * rows_per_tile + r
        h_t = lax.dot_general(u_ref[r * N_KEYS:(r + 1) * N_KEYS, :], xb, NT_DIMS,
                              preferred_element_type=F32)
        gate = jnp.zeros(h_t.shape, F32)
        for h in range(PEER_HEADS):
            s1 = s1_ref[h, pl.ds(a_idx, 1), :]
            c1 = c1_ref[h, pl.ds(a_idx, 1), :]
            gate = gate + jnp.where(s1 + s2_ref[h] >= tau_ref[h], c1 * e2_ref[h], 0.0)
        act = gate * (0.5 * h_t * (1.0 + lax.erf(h_t * (2.0 ** -0.5))))
        a_ref[r * N_KEYS:(r + 1) * N_KEYS, :] = act.astype(BF16)
    y_ref[...] += lax.dot_general(a_ref[...], v_ref[...], TN_DIMS, preferred_element_type=F32)


def _peer_experts(xb, route, u, v, tm, te):
    L, D = xb.shape
    s1, c1, s2, e2, tau = route
    n_exp = u.shape[0]
    big_spec = pl.BlockSpec((PEER_HEADS, N_KEYS, tm), lambda i, j: (0, 0, i))
    return pl.pallas_call(
        functools.partial(_peer_kernel, rows_per_tile=te // N_KEYS),
        grid=(L // tm, n_exp // te),
        in_specs=[pl.BlockSpec((tm, D), lambda i, j: (i, 0)),
                  big_spec, big_spec, big_spec, big_spec,
                  pl.BlockSpec((PEER_HEADS, 1, tm), lambda i, j: (0, 0, i)),
                  pl.BlockSpec((te, D), lambda i, j: (j, 0)),
                  pl.BlockSpec((te, D), lambda i, j: (j, 0))],
        out_specs=pl.BlockSpec((tm, D), lambda i, j: (i, 0)),
        out_shape=jax.ShapeDtypeStruct((L, D), F32),
        scratch_shapes=[pltpu.VMEM((te, tm), BF16)],
        compiler_params=_cparams("parallel", "arbitrary"),
        name="peer_experts",
    )(xb, s1, c1, s2, e2, tau, u, v)


def _add_ln_kernel(x_ref, y_ref, g_ref, b_ref, o_ref, ob_ref):
    out = _layer_norm(ALPHA * x_ref[...] + y_ref[...], g_ref[...], b_ref[...])
    o_ref[...] = out
    ob_ref[...] = out.astype(BF16)


def _add_ln(x, y, g, b, tm):
    L, D = x.shape
    row = lambda i: (i, 0)
    fixed = lambda i: (0, 0)
    return pl.pallas_call(
        _add_ln_kernel,
        grid=(L // tm,),
        in_specs=[pl.BlockSpec((tm, D), row), pl.BlockSpec((tm, D), row),
                  pl.BlockSpec((1, D), fixed), pl.BlockSpec((1, D), fixed)],
        out_specs=(pl.BlockSpec((tm, D), row), pl.BlockSpec((tm, D), row)),
        out_shape=(jax.ShapeDtypeStruct((L, D), F32), jax.ShapeDtypeStruct((L, D), BF16)),
        compiler_params=_cparams("parallel"),
        name="residual_ln",
    )(x, y, g.reshape(1, D), b.reshape(1, D))


def _fold(t, d):
    L, W = t.shape
    return t.reshape(L // d, d, W).transpose(1, 0, 2)


def _unfold(t):
    d, Lf, W = t.shape
    return t.transpose(1, 0, 2).reshape(Lf * d, W)


def _prep_in_weights(w_in):
    cols = lambda s: w_in[:, IN_OFFSETS[s]:IN_OFFSETS[s + 1]]
    w_rope = jnp.concatenate([cols(s) for s in ROPE_SLOTS], 1).astype(BF16)
    pad = jnp.zeros((w_in.shape[0], LANES - IN_SPLITS[SL_DV]), w_in.dtype)
    w_val = jnp.concatenate([cols(SL_BV), cols(SL_CV), cols(SL_AV), cols(SL_DV), pad], 1).astype(BF16)
    pad = jnp.zeros((w_in.shape[0], LANES - IN_SPLITS[SL_IW]), w_in.dtype)
    w_iw = jnp.concatenate([cols(SL_IW), pad], 1).astype(BF16)
    return w_rope, w_val, w_iw


def _mixer_layer(x, xin, tables, w_in, sinks, lam_p, sub_g, w_out, ln_g, ln_b, layer_idx, cfg):
    L = x.shape[0]
    w_rope, w_val, w_iw = _prep_in_weights(w_in)
    rp = _project(xin, w_rope, tables, BF16, cfg["proj_tm"], cfg["proj_tn"])
    vp = _project(xin, w_val, None, BF16, cfg["proj_tm"], VAL_WIDTH // 2)
    iw = _project(xin, w_iw, None, F32, cfg["proj_tm"], LANES)

    oa = _banded(rp[None], ROPE_OFF[SL_AQ] // 512, rp[None], ROPE_OFF[SL_AK] // LANES,
                 vp[None], VAL_OFF[SL_AV] // LANES, A_HEADS, A_KV_HEADS, SWA_WINDOW - 1,
                 sinks=sinks)[0]

    lam_init = 0.8 - 0.6 * math.exp(-0.3 * layer_idx)
    ob = _diff_attention(rp, vp, lam_p, sub_g, lam_init, cfg["diff_tq"])

    cq = rp[:, ROPE_OFF[SL_CQ]:ROPE_OFF[SL_CQ] + 512]
    ck = rp[:, ROPE_OFF[SL_CK]:ROPE_OFF[SL_CK] + 512]
    cv = vp[:, VAL_OFF[SL_CV]:VAL_OFF[SL_CV] + 512]
    ocs, lses = [], []
    for window, dil in DILATION_PAIRS:
        o_c, lse_c = _banded(_fold(cq, dil), 0, _fold(ck, dil), 0, _fold(cv, dil), 0,
                             C_HEADS, C_HEADS, window // dil)
        ocs.append(_unfold(o_c))
        lses.append(_unfold(lse_c))

    dk = rp[:, ROPE_OFF[SL_DK]:ROPE_OFF[SL_DK] + HEAD_DIM]
    ik = rp[:, ROPE_OFF[SL_IK]:ROPE_OFF[SL_IK] + HEAD_DIM]
    dv = vp[:, VAL_OFF[SL_DV]:VAL_OFF[SL_DV] + HEAD_DIM]
    od = _dsa_attention(rp, iw, dk, ik, dv, cfg["dsa_tk"])

    return _out_projection(x, oa, ob, ocs, lses, od, w_out.astype(BF16), ln_g, ln_b,
                           cfg["out_tm"])


def _peer_layer(x, xb, wq, keys, u, v, ln_g, ln_b, cfg):
    route = _peer_route(xb, wq.astype(BF16),
                        keys.reshape(2 * PEER_HEADS, N_KEYS, PEER_HALF).astype(BF16),
                        cfg["route_tm"])
    y = _peer_experts(xb, route, u.astype(BF16), v.astype(BF16), cfg["peer_tm"], cfg["peer_te"])
    return _add_ln(x, y, ln_g, ln_b, cfg["out_tm"])


def _config(L):
    return dict(proj_tm=min(512, L), proj_tn=768, diff_tq=min(512, L), dsa_tk=min(512, L),
                out_tm=min(256, L), route_tm=min(256, L), peer_tm=min(512, L), peer_te=512)


def _trunk(x2, w_in, attn_sinks, diff_lambda, diff_norm_g, w_out, ln1_g, ln1_b,
           peer_wq, peer_keys, peer_u, peer_v, ln2_g, ln2_b):
    L = x2.shape[0]
    cfg = _config(L)
    tables = _rope_tables(L, cfg["proj_tn"])
    xin = x2
    for i in range(w_in.shape[0]):
        x2, xb = _mixer_layer(x2, xin, tables, w_in[i], attn_sinks[i], diff_lambda[i],
                              diff_norm_g[i], w_out[i], ln1_g[i], ln1_b[i], i, cfg)
        x2, xb = _peer_layer(x2, xb, peer_wq[i], peer_keys[i], peer_u[i], peer_v[i],
                             ln2_g[i], ln2_b[i], cfg)
        xin = xb
    return x2


def kernel(x, w_in, attn_sinks, diff_lambda, diff_norm_g, w_out, ln1_g, ln1_b,
           peer_wq, peer_keys, peer_u, peer_v, ln2_g, ln2_b):
    b, L, D = x.shape
    outs = [_trunk(x[i], w_in, attn_sinks, diff_lambda, diff_norm_g, w_out, ln1_g, ln1_b,
                   peer_wq, peer_keys, peer_u, peer_v, ln2_g, ln2_b) for i in range(b)]
    return jnp.stack(outs, 0)
```

```python
import functools
import math

import jax
import jax.numpy as jnp
import numpy as np
from jax import lax
from jax.experimental import pallas as pl
from jax.experimental.pallas import tpu as pltpu

F32 = jnp.float32
BF16 = jnp.bfloat16
I32 = jnp.int32

D_MODEL = 2048
DEPTH = 2
HEAD_DIM = 64
HALF_DIM = HEAD_DIM // 2
LANES = 128
BLOCK = 128
NEG_INF = -1e30
ROPE_THETA = 10000.0
LN_EPS = 1e-5
RMS_EPS = 1e-5
A_HEADS, A_KV_HEADS, SWA_WINDOW = 8, 2, 128
B_HEADS = 4
C_HEADS = 8
DILATION_PAIRS = ((128, 1), (512, 4), (2048, 16))
D_HEADS, IDX_HEADS, IDX_DIM = 8, 8, 64
DSA_TOPK_MAX = 256
PEER_HEADS, PEER_TOPK, N_KEYS = 8, 16, 128
PEER_QDIM = 256
PEER_HALF = PEER_QDIM // 2
ALPHA = (2 * DEPTH) ** 0.25
Q_SCALE = HEAD_DIM ** -0.5
INT_MIN = -2 ** 31

IN_SPLITS = (512, 128, 128, 512, 512, 512, 512, 512, 512, 512, 64, 64, 512, 64, 8)
IN_OFFSETS = tuple(int(v) for v in np.concatenate([[0], np.cumsum(IN_SPLITS)]))
(SL_AQ, SL_AK, SL_AV, SL_BQ, SL_BK, SL_BV, SL_CQ, SL_CK, SL_CV,
 SL_DQ, SL_DK, SL_DV, SL_IQ, SL_IK, SL_IW) = range(15)
ROPE_SLOTS = (SL_AQ, SL_BQ, SL_BK, SL_CQ, SL_CK, SL_DQ, SL_IQ, SL_AK, SL_DK, SL_IK)
ROPE_OFF = {}
_o = 0
for _s in ROPE_SLOTS:
    ROPE_OFF[_s] = _o
    _o += IN_SPLITS[_s]
ROPE_WIDTH = _o
VAL_OFF = {SL_BV: 0, SL_CV: 512, SL_AV: 1024, SL_DV: 1152}
VAL_WIDTH = 1280
VMEM_LIMIT = 56 * 1024 * 1024

NT_DIMS = (((1,), (1,)), ((), ()))
TN_DIMS = (((0,), (0,)), ((), ()))


def _cparams(*sem):
    return pltpu.CompilerParams(dimension_semantics=sem, vmem_limit_bytes=VMEM_LIMIT)


def _proj_kernel(x_ref, w_ref, *rest, rope):
    if rope:
        cos_ref, sa_ref, sb_ref, o_ref, xb_ref = rest
    else:
        o_ref, xb_ref = rest

    @pl.when(pl.program_id(1) == 0)
    def _():
        xb_ref[...] = x_ref[...].astype(BF16)

    y = jnp.dot(xb_ref[...], w_ref[...], preferred_element_type=F32)
    if rope:
        for t in range(y.shape[1] // LANES):
            sl = slice(t * LANES, (t + 1) * LANES)
            yt = y[:, sl]
            o_ref[:, sl] = (yt * cos_ref[:, sl]
                            + pltpu.roll(yt, LANES - HALF_DIM, 1) * sa_ref[:, sl]
                            + pltpu.roll(yt, HALF_DIM, 1) * sb_ref[:, sl]).astype(o_ref.dtype)
    else:
        o_ref[...] = y.astype(o_ref.dtype)


def _project(x, w, tables, out_dtype, tm, tn):
    L, K = x.shape
    N = w.shape[1]
    rope = tables is not None
    in_specs = [pl.BlockSpec((tm, K), lambda i, j: (i, 0)),
                pl.BlockSpec((K, tn), lambda i, j: (0, j))]
    args = [x, w]
    if rope:
        in_specs += [pl.BlockSpec((tm, tn), lambda i, j: (i, 0))] * 3
        args += list(tables)
    return pl.pallas_call(
        functools.partial(_proj_kernel, rope=rope),
        grid=(L // tm, N // tn),
        in_specs=in_specs,
        out_specs=pl.BlockSpec((tm, tn), lambda i, j: (i, j)),
        out_shape=jax.ShapeDtypeStruct((L, N), out_dtype),
        scratch_shapes=[pltpu.VMEM((tm, K), BF16)],
        compiler_params=_cparams("parallel", "arbitrary"),
        name="proj_rope" if rope else "proj_plain",
    )(*args)


def _rope_tables(L, width):
    inv_freq = ROPE_THETA ** (-jnp.arange(HALF_DIM, dtype=F32) / HALF_DIM)
    ang = jnp.arange(L, dtype=F32)[:, None] * inv_freq[None, :]
    cos, sin = jnp.cos(ang), jnp.sin(ang)
    zero = jnp.zeros_like(sin)
    reps = width // HEAD_DIM
    cos_t = jnp.tile(jnp.concatenate([cos, cos], 1), (1, reps))
    sin_a = jnp.tile(jnp.concatenate([-sin, zero], 1), (1, reps))
    sin_b = jnp.tile(jnp.concatenate([zero, sin], 1), (1, reps))
    return cos_t, sin_a, sin_b


def _banded_kernel(*refs, n_kv, group, max_dist, use_sink):
    if use_sink:
        sink_ref, q_ref, kp_ref, kc_ref, vp_ref, vc_ref, o_ref = refs
    else:
        q_ref, kp_ref, kc_ref, vp_ref, vc_ref, o_ref, lse_ref = refs
    n = pl.program_id(1)
    q = q_ref[...] * Q_SCALE
    k2 = jnp.concatenate([kp_ref[...], kc_ref[...]], 0)
    v2 = jnp.concatenate([vp_ref[...], vc_ref[...]], 0)
    row = lax.broadcasted_iota(I32, (BLOCK, 2 * BLOCK), 0)
    col = lax.broadcasted_iota(I32, (BLOCK, 2 * BLOCK), 1)
    dist = row + BLOCK - col
    lo = jnp.where(n > 0, 0, BLOCK)
    ok = (dist >= 0) & (dist <= max_dist) & (col >= lo)
    lane = lax.broadcasted_iota(I32, (BLOCK, LANES), 1)
    outs = []
    lse_tile = jnp.zeros((BLOCK, LANES), F32)
    for g in range(n_kv):
        kh = k2[:, g * HEAD_DIM:(g + 1) * HEAD_DIM]
        vh = v2[:, g * HEAD_DIM:(g + 1) * HEAD_DIM]
        for u in range(group):
            h = g * group + u
            qh = q[:, h * HEAD_DIM:(h + 1) * HEAD_DIM]
            s = lax.dot_general(qh, kh, NT_DIMS, preferred_element_type=F32)
            s = jnp.where(ok, s, NEG_INF)
            m = jnp.max(s, axis=1, keepdims=True)
            if use_sink:
                m = jnp.maximum(m, sink_ref[h])
            p = jnp.exp(s - m)
            l = jnp.sum(p, axis=1, keepdims=True)
            if use_sink:
                l = l + jnp.exp(sink_ref[h] - m)
            acc = jnp.dot(p.astype(BF16), vh, preferred_element_type=F32)
            outs.append(acc / l)
            if not use_sink:
                lse_tile = jnp.where(lane == h, m + jnp.log(l), lse_tile)
    o_ref[...] = jnp.concatenate(outs, 1).astype(o_ref.dtype)
    if not use_sink:
        lse_ref[...] = lse_tile


def _banded(q3, qcol, k3, kcol, v3, vcol, n_heads, n_kv, max_dist, sinks=None):
    d, Lf, _ = q3.shape
    nb = Lf // BLOCK
    wq, wk = n_heads * HEAD_DIM, n_kv * HEAD_DIM
    use_sink = sinks is not None
    cur = lambda c: (lambda r, n: (r, n, c))
    prev = lambda c: (lambda r, n: (r, jnp.maximum(n - 1, 0), c))
    in_specs = [pl.BlockSpec((None, BLOCK, wq), cur(qcol)),
                pl.BlockSpec((None, BLOCK, wk), prev(kcol)),
                pl.BlockSpec((None, BLOCK, wk), cur(kcol)),
                pl.BlockSpec((None, BLOCK, wk), prev(vcol)),
                pl.BlockSpec((None, BLOCK, wk), cur(vcol))]
    args = [q3, k3, k3, v3, v3]
    if use_sink:
        in_specs = [pl.BlockSpec(memory_space=pltpu.SMEM)] + in_specs
        args = [sinks] + args
        out_shape = jax.ShapeDtypeStruct((d, Lf, wq), BF16)
        out_specs = pl.BlockSpec((None, BLOCK, wq), cur(0))
    else:
        out_shape = (jax.ShapeDtypeStruct((d, Lf, wq), F32),
                     jax.ShapeDtypeStruct((d, Lf, LANES), F32))
        out_specs = (pl.BlockSpec((None, BLOCK, wq), cur(0)),
                     pl.BlockSpec((None, BLOCK, LANES), cur(0)))
    return pl.pallas_call(
        functools.partial(_banded_kernel, n_kv=n_kv, group=n_heads // n_kv,
                          max_dist=max_dist, use_sink=use_sink),
        grid=(d, nb),
        in_specs=in_specs,
        out_specs=out_specs,
        out_shape=out_shape,
        compiler_params=_cparams("parallel", "parallel"),
        name="swa_sink" if use_sink else "dilated_band",
    )(*args)


def _diff_kernel(lam_ref, g_ref, q_ref, k_ref, v_ref, o_ref, m_ref, acc_ref, *, tq, tk,
                 lam_init):
    i = pl.program_id(1)
    lane = lax.broadcasted_iota(I32, (tq, LANES), 1)
    q = q_ref[...] * Q_SCALE
    zero = jnp.zeros_like(q)
    qz = (jnp.where(lane < HEAD_DIM, q, zero), jnp.where(lane >= HEAD_DIM, q, zero))
    hq = tq // 2
    chains = [(s, r) for s in range(2) for r in range(2)]
    qc = [qz[s][r * hq:(r + 1) * hq] for s, r in chains]
    m_ref[...] = jnp.full(m_ref.shape, NEG_INF, F32)
    acc_ref[...] = jnp.zeros(acc_ref.shape, F32)
    ones_col = jnp.where(lax.broadcasted_iota(I32, (tk, LANES), 1) == 0, 1.0, 0.0).astype(BF16)

    def tile(j, masked):
        start = pl.multiple_of(j * tk, tk)
        kt = k_ref[pl.ds(start, tk), :]
        v_ext = jnp.concatenate([v_ref[pl.ds(start, tk), :], ones_col], 1)

        def scores(c):
            s = lax.dot_general(qc[c], kt, NT_DIMS, preferred_element_type=F32)
            if masked:
                qpos = i * tq + chains[c][1] * hq + lax.broadcasted_iota(I32, (hq, tk), 0)
                kpos = start + lax.broadcasted_iota(I32, (hq, tk), 1)
                s = jnp.where(kpos <= qpos, s, NEG_INF)
            return s

        s_next = scores(0)
        for c in range(len(chains)):
            s = s_next
            if c + 1 < len(chains):
                s_next = scores(c + 1)
            m_old = m_ref[c]
            m_new = jnp.maximum(m_old, jnp.max(s, axis=1, keepdims=True))
            p = jnp.exp(s - m_new).astype(BF16)
            acc_ref[c] = (jnp.exp(m_old - m_new) * acc_ref[c]
                          + jnp.dot(p, v_ext, preferred_element_type=F32))
            m_ref[c] = m_new

    def body(j, carry):
        tile(j, False)
        return carry

    n_full = (i * tq) // tk
    lax.fori_loop(0, n_full, body, 0)
    tile(n_full, True)

    lp = lam_ref[...]
    lam = (jnp.exp(jnp.sum(lp[0:1] * lp[1:2], axis=1, keepdims=True))
           - jnp.exp(jnp.sum(lp[2:3] * lp[3:4], axis=1, keepdims=True)) + lam_init)
    o = [acc_ref[c][:, :LANES] / acc_ref[c][:, LANES:LANES + 1] for c in range(len(chains))]
    ob = jnp.concatenate(o[0:2], 0) - lam * jnp.concatenate(o[2:4], 0)
    ob = ob * lax.rsqrt(jnp.mean(ob * ob, axis=1, keepdims=True) + RMS_EPS)
    o_ref[...] = (ob * g_ref[...] * (1.0 - lam_init)).astype(o_ref.dtype)


def _diff_attention(rp, vp, lam_p, sub_g, lam_init, tq, tk):
    L = rp.shape[0]
    qcol = ROPE_OFF[SL_BQ] // LANES
    kcol = ROPE_OFF[SL_BK] // LANES
    vcol = VAL_OFF[SL_BV] // LANES
    return pl.pallas_call(
        functools.partial(_diff_kernel, tq=tq, tk=tk, lam_init=lam_init),
        grid=(B_HEADS, L // tq),
        in_specs=[pl.BlockSpec((4, HEAD_DIM), lambda h, i: (0, 0)),
                  pl.BlockSpec((1, LANES), lambda h, i: (0, 0)),
                  pl.BlockSpec((tq, LANES), lambda h, i: (i, qcol + h)),
                  pl.BlockSpec((L, LANES), lambda h, i: (0, kcol + h)),
                  pl.BlockSpec((L, LANES), lambda h, i: (0, vcol + h))],
        out_specs=pl.BlockSpec((tq, LANES), lambda h, i: (i, h)),
        out_shape=jax.ShapeDtypeStruct((L, B_HEADS * LANES), BF16),
        scratch_shapes=[pltpu.VMEM((4, tq // 2, 1), F32),
                        pltpu.VMEM((4, tq // 2, 2 * LANES), F32)],
        compiler_params=_cparams("parallel", "arbitrary"),
        name="diff_attn",
    )(lam_p, sub_g.reshape(1, LANES), rp, rp, vp)


def _sortable(score):
    b = pltpu.bitcast(score, I32)
    return jnp.where(b >= 0, b, b ^ 0x7FFFFFFF)


def _dsa_kernel(q_ref, iq_ref, iw_ref, k_ref, ik_ref, v_ref, o_ref, key_ref, m_ref, acc_ref, *,
                tk, topk):
    n = pl.program_id(0)
    n_chunks = (n * BLOCK + BLOCK + tk - 1) // tk
    qpos = n * BLOCK + lax.broadcasted_iota(I32, (BLOCK, tk), 0)
    col = lax.broadcasted_iota(I32, (BLOCK, tk), 1)

    def stack_heads(t):
        return jnp.concatenate([t[:, h * HEAD_DIM:(h + 1) * HEAD_DIM] for h in range(D_HEADS)], 0)

    iq8 = stack_heads(iq_ref[...] * (IDX_DIM ** -0.5))
    iw = iw_ref[...] * (IDX_HEADS ** -0.5)
    w8 = [iw[:, h:h + 1] for h in range(IDX_HEADS)]

    def score_chunk(c, carry):
        start = pl.multiple_of(c * tk, tk)
        sc = lax.dot_general(iq8, ik_ref[pl.ds(start, tk), :], NT_DIMS,
                             preferred_element_type=F32)
        score = jnp.zeros((BLOCK, tk), F32)
        for h in range(IDX_HEADS):
            score = score + w8[h] * jnp.maximum(sc[h * BLOCK:(h + 1) * BLOCK], 0.0)
        score = jnp.where(start + col <= qpos, score, NEG_INF)
        key_ref[c] = _sortable(score)
        return carry

    lax.fori_loop(0, n_chunks, score_chunk, 0)

    thr = jnp.full((BLOCK, 1), INT_MIN, I32)
    for bit in range(31, -1, -1):
        cand = thr + np.int32(INT_MIN if bit == 31 else 1 << bit)

        def count_chunk(c, cnt, cand=cand):
            hit = jnp.where(key_ref[c] >= cand, 1, 0)
            for u in range(tk // LANES):
                cnt = cnt + hit[:, u * LANES:(u + 1) * LANES]
            return cnt

        cnt = lax.fori_loop(0, n_chunks, count_chunk, jnp.zeros((BLOCK, LANES), I32))
        thr = jnp.where(jnp.sum(cnt, axis=1, keepdims=True) >= topk, cand, thr)

    per = 2
    n_chain = D_HEADS // per
    q_all = q_ref[...] * Q_SCALE
    qc = [jnp.concatenate([q_all[:, h * HEAD_DIM:(h + 1) * HEAD_DIM]
                           for h in range(g * per, (g + 1) * per)], 0) for g in range(n_chain)]
    m_ref[...] = jnp.full(m_ref.shape, NEG_INF, F32)
    acc_ref[...] = jnp.zeros(acc_ref.shape, F32)

    def attend_chunk(c, carry):
        start = pl.multiple_of(c * tk, tk)
        keep = (key_ref[c] >= thr) & (start + col <= qpos)
        bias = jnp.where(keep, 0.0, NEG_INF)[None]
        kt = k_ref[pl.ds(start, tk), :]
        vt = v_ref[pl.ds(start, tk), :]

        def scores(g):
            s = lax.dot_general(qc[g], kt, NT_DIMS, preferred_element_type=F32)
            return s.reshape(per, BLOCK, tk) + bias

        s_next = scores(0)
        for g in range(n_chain):
            s = s_next
            if g + 1 < n_chain:
                s_next = scores(g + 1)
            m_old = m_ref[g]
            m_new = jnp.maximum(m_old, jnp.max(s, axis=2, keepdims=True))
            p = jnp.exp(s - jnp.maximum(m_new, 0.1 * NEG_INF))
            pv = jnp.dot(p.reshape(per * BLOCK, tk).astype(BF16), vt, preferred_element_type=F32)
            acc_ref[g] = jnp.exp(m_old - m_new) * acc_ref[g] + pv.reshape(per, BLOCK, LANES)
            m_ref[g] = m_new
        return carry

    lax.fori_loop(0, n_chunks, attend_chunk, 0)
    outs = []
    for g in range(n_chain):
        acc = acc_ref[g]
        out = acc[:, :, :HEAD_DIM] / acc[:, :, HEAD_DIM:HEAD_DIM + 1]
        outs += [out[u] for u in range(per)]
    o_ref[...] = jnp.concatenate(outs, 1).astype(o_ref.dtype)


def _dsa_attention(rp, iw, dk, ik, dv, tk):
    L = rp.shape[0]
    topk = min(DSA_TOPK_MAX, L // 4)
    wide = D_HEADS * HEAD_DIM
    return pl.pallas_call(
        functools.partial(_dsa_kernel, tk=tk, topk=topk),
        grid=(L // BLOCK,),
        in_specs=[pl.BlockSpec((BLOCK, wide), lambda n: (n, ROPE_OFF[SL_DQ] // wide)),
                  pl.BlockSpec((BLOCK, wide), lambda n: (n, ROPE_OFF[SL_IQ] // wide)),
                  pl.BlockSpec((BLOCK, LANES), lambda n: (n, 0)),
                  pl.BlockSpec((L, HEAD_DIM), lambda n: (0, 0)),
                  pl.BlockSpec((L, HEAD_DIM), lambda n: (0, 0)),
                  pl.BlockSpec((L, LANES), lambda n: (0, 0))],
        out_specs=pl.BlockSpec((BLOCK, wide), lambda n: (n, 0)),
        out_shape=jax.ShapeDtypeStruct((L, wide), BF16),
        scratch_shapes=[pltpu.VMEM((L // tk, BLOCK, tk), I32),
                        pltpu.VMEM((D_HEADS // 2, 2, BLOCK, 1), F32),
                        pltpu.VMEM((D_HEADS // 2, 2, BLOCK, LANES), F32)],
        compiler_params=_cparams("parallel"),
        name="dsa_attn",
    )(rp, rp, iw, dk, ik, dv)


def _layer_norm(z, g, b):
    mu = jnp.mean(z, axis=1, keepdims=True)
    zc = z - mu
    var = jnp.mean(zc * zc, axis=1, keepdims=True)
    return zc * lax.rsqrt(var + LN_EPS) * g + b


def _outproj_kernel(x_ref, oa_ref, ob_ref, oc1_ref, oc2_ref, oc3_ref, ls1_ref, ls2_ref, ls3_ref,
                    od_ref, w_ref, g_ref, b_ref, y_ref, yb_ref):
    ocs = (oc1_ref[...], oc2_ref[...], oc3_ref[...])
    lses = (ls1_ref[...], ls2_ref[...], ls3_ref[...])
    top = jnp.maximum(jnp.maximum(lses[0], lses[1]), lses[2])
    wts = [jnp.exp(t - top) for t in lses]
    inv = 1.0 / (wts[0] + wts[1] + wts[2])
    merged = []
    for h in range(C_HEADS):
        sl = slice(h * HEAD_DIM, (h + 1) * HEAD_DIM)
        num = sum(wts[c][:, h:h + 1] * ocs[c][:, sl] for c in range(3))
        merged.append(num * inv[:, h:h + 1])
    oc = jnp.concatenate(merged, 1).astype(BF16)
    mixed = jnp.concatenate([oa_ref[...], ob_ref[...], oc, od_ref[...]], 1)
    y = jnp.dot(mixed, w_ref[...], preferred_element_type=F32)
    out = _layer_norm(ALPHA * x_ref[...] + y, g_ref[...], b_ref[...])
    y_ref[...] = out
    yb_ref[...] = out.astype(BF16)


def _out_projection(x, oa, ob, ocs, lses, od, w, g, b, tm):
    L, D = x.shape
    gw = D // 4
    row = lambda i: (i, 0)
    fixed = lambda i: (0, 0)
    return pl.pallas_call(
        _outproj_kernel,
        grid=(L // tm,),
        in_specs=[pl.BlockSpec((tm, D), row), pl.BlockSpec((tm, gw), row),
                  pl.BlockSpec((tm, gw), row)]
                 + [pl.BlockSpec((tm, gw), row)] * 3 + [pl.BlockSpec((tm, LANES), row)] * 3
                 + [pl.BlockSpec((tm, gw), row), pl.BlockSpec((D, D), fixed),
                    pl.BlockSpec((1, D), fixed), pl.BlockSpec((1, D), fixed)],
        out_specs=(pl.BlockSpec((tm, D), row), pl.BlockSpec((tm, D), row)),
        out_shape=(jax.ShapeDtypeStruct((L, D), F32), jax.ShapeDtypeStruct((L, D), BF16)),
        compiler_params=_cparams("parallel"),
        name="outproj_ln",
    )(x, oa, ob, *ocs, *lses, od, w, g.reshape(1, D), b.reshape(1, D))


def _top_values(t, k):
    vals = []
    for _ in range(k):
        mx = jnp.max(t, axis=0, keepdims=True)
        vals.append(mx)
        t = jnp.where(t == mx, -jnp.inf, t)
    return vals


def _route_kernel(xb_ref, wq_ref, keys_ref, th_ref, c1_ref, s2_ref, e2_ref):
    tm = xb_ref.shape[0]
    q = jnp.dot(xb_ref[...], wq_ref[...], preferred_element_type=F32).astype(BF16)
    for h in range(PEER_HEADS):
        st = []
        for p in range(2):
            lo = (2 * h + p) * PEER_HALF
            st.append(lax.dot_general(keys_ref[2 * h + p], q[:, lo:lo + PEER_HALF], NT_DIMS,
                                      preferred_element_type=F32))
        v1 = _top_values(st[0], PEER_TOPK)
        v2 = _top_values(st[1], PEER_TOPK)
        v2_all = jnp.concatenate(v2, 0)
        cand = jnp.concatenate([v1[i] + v2_all for i in range(PEER_TOPK)], 0)
        mu = _top_values(cand, PEER_TOPK)
        z = sum(jnp.exp(m - mu[0]) for m in mu)
        tau = mu[PEER_TOPK - 1]
        th = jnp.full(st[0].shape, jnp.inf, F32)
        for j in range(PEER_TOPK):
            th = jnp.minimum(th, jnp.where(st[0] + v2[j] >= tau, v2[j], jnp.inf))
        th_ref[h] = th.reshape(N_KEYS // 8, 8, tm)
        c1_ref[h] = (jnp.exp(st[0] - v1[0]) / z).reshape(N_KEYS // 8, 8, tm)
        s2_ref[h] = st[1]
        e2_ref[h] = jnp.exp(st[1] - v2[0])


def _peer_route(xb, wq, keys, tm):
    L, D = xb.shape
    big = jax.ShapeDtypeStruct((PEER_HEADS, N_KEYS, L), F32)
    big_spec = pl.BlockSpec((PEER_HEADS, N_KEYS, tm), lambda i: (0, 0, i))
    rows = jax.ShapeDtypeStruct((PEER_HEADS, N_KEYS // 8, 8, L), F32)
    rows_spec = pl.BlockSpec((PEER_HEADS, N_KEYS // 8, 8, tm), lambda i: (0, 0, 0, i))
    return pl.pallas_call(
        _route_kernel,
        grid=(L // tm,),
        in_specs=[pl.BlockSpec((tm, D), lambda i: (i, 0)),
                  pl.BlockSpec((D, PEER_HEADS * PEER_QDIM), lambda i: (0, 0)),
                  pl.BlockSpec((2 * PEER_HEADS, N_KEYS, PEER_HALF), lambda i: (0, 0, 0))],
        out_specs=(rows_spec, rows_spec, big_spec, big_spec),
        out_shape=(rows, rows, big, big),
        compiler_params=_cparams("parallel"),
        name="peer_route",
    )(xb, wq, keys)


PEER_ROWS = 4
PEER_TE = PEER_ROWS * N_KEYS


def _peer_kernel(xb_ref, th_ref, c1_ref, s2_ref, e2_ref, u_ref, v_ref, y_ref, a_ref, h_ref,
                 ga_ref, gb_ref):
    j = pl.program_id(1)
    tm = xb_ref.shape[0]
    d_model = v_ref.shape[1]
    n_lane_tiles = tm // LANES
    pieces = [(r, c) for r in range(PEER_ROWS) for c in range(n_lane_tiles)]

    def gate_piece(g_ref, a_hi, a_lo, r, c):
        ln = slice(c * LANES, (c + 1) * LANES)
        acc = jnp.zeros((N_KEYS, LANES), F32)
        for h in range(PEER_HEADS):
            th = th_ref[h, a_hi, a_lo:a_lo + 1, ln]
            c1 = c1_ref[h, a_hi, a_lo:a_lo + 1, ln]
            acc = acc + jnp.where(s2_ref[h, :, ln] >= th, c1 * e2_ref[h, :, ln], 0.0)
        g_ref[r * N_KEYS:(r + 1) * N_KEYS, ln] = acc

    @pl.when(j == 0)
    def _():
        y_ref[...] = jnp.zeros(y_ref.shape, F32)
        for r, c in pieces:
            gate_piece(ga_ref, 0, r, r, c)

    def step(g_cur, g_next, parity):
        a_hi = jnp.minimum((j + 1) // 2, N_KEYS // 8 - 1)
        a_lo0 = PEER_ROWS * (1 - parity)
        todo = list(pieces)

        def gates(count):
            for _ in range(count):
                r, c = todo.pop(0)
                gate_piece(g_next, a_hi, a_lo0 + r, r, c)

        n_split = 4
        per_dot = len(pieces) // (2 * n_split)
        kc = d_model // n_split
        h_t = None
        for q in range(n_split):
            part = lax.dot_general(u_ref[:, q * kc:(q + 1) * kc], xb_ref[:, q * kc:(q + 1) * kc],
                                   NT_DIMS, preferred_element_type=F32)
            h_t = part if h_t is None else part + h_t
            gates(per_dot)
        h_ref[...] = h_t
        for r, c in pieces:
            rows = slice(r * N_KEYS, (r + 1) * N_KEYS)
            ln = slice(c * LANES, (c + 1) * LANES)
            hh = h_ref[rows, ln]
            act = g_cur[rows, ln] * (0.5 * hh * (1.0 + lax.erf(hh * (2.0 ** -0.5))))
            a_ref[rows, ln] = act.astype(BF16)
        dc = d_model // n_split
        for q in range(n_split):
            cols = slice(q * dc, (q + 1) * dc)
            y_ref[:, cols] += lax.dot_general(a_ref[...], v_ref[:, cols], TN_DIMS,
                                              preferred_element_type=F32)
            gates(per_dot)
        gates(len(todo))

    @pl.when(j % 2 == 0)
    def _():
        step(ga_ref, gb_ref, 0)

    @pl.when(j % 2 == 1)
    def _():
        step(gb_ref, ga_ref, 1)


def _peer_experts(xb, route, u, v, tm):
    L, D = xb.shape
    th, c1, s2, e2 = route
    n_exp = u.shape[0]
    assert (n_exp // PEER_TE) % 2 == 0 and 2 * PEER_ROWS == 8
    big_spec = pl.BlockSpec((PEER_HEADS, N_KEYS, tm), lambda i, j: (0, 0, i))
    rows_spec = pl.BlockSpec((PEER_HEADS, N_KEYS // 8, 8, tm), lambda i, j: (0, 0, 0, i))
    return pl.pallas_call(
        _peer_kernel,
        grid=(L // tm, n_exp // PEER_TE),
        in_specs=[pl.BlockSpec((tm, D), lambda i, j: (i, 0)),
                  rows_spec, rows_spec, big_spec, big_spec,
                  pl.BlockSpec((PEER_TE, D), lambda i, j: (j, 0)),
                  pl.BlockSpec((PEER_TE, D), lambda i, j: (j, 0))],
        out_specs=pl.BlockSpec((tm, D), lambda i, j: (i, 0)),
        out_shape=jax.ShapeDtypeStruct((L, D), F32),
        scratch_shapes=[pltpu.VMEM((PEER_TE, tm), BF16), pltpu.VMEM((PEER_TE, tm), F32),
                        pltpu.VMEM((PEER_TE, tm), F32), pltpu.VMEM((PEER_TE, tm), F32)],
        compiler_params=_cparams("parallel", "arbitrary"),
        name="peer_experts",
    )(xb, th, c1, s2, e2, u, v)


def _add_ln_kernel(x_ref, y_ref, g_ref, b_ref, o_ref, ob_ref):
    out = _layer_norm(ALPHA * x_ref[...] + y_ref[...], g_ref[...], b_ref[...])
    o_ref[...] = out
    ob_ref[...] = out.astype(BF16)


def _add_ln(x, y, g, b, tm):
    L, D = x.shape
    row = lambda i: (i, 0)
    fixed = lambda i: (0, 0)
    return pl.pallas_call(
        _add_ln_kernel,
        grid=(L // tm,),
        in_specs=[pl.BlockSpec((tm, D), row), pl.BlockSpec((tm, D), row),
                  pl.BlockSpec((1, D), fixed), pl.BlockSpec((1, D), fixed)],
        out_specs=(pl.BlockSpec((tm, D), row), pl.BlockSpec((tm, D), row)),
        out_shape=(jax.ShapeDtypeStruct((L, D), F32), jax.ShapeDtypeStruct((L, D), BF16)),
        compiler_params=_cparams("parallel"),
        name="residual_ln",
    )(x, y, g.reshape(1, D), b.reshape(1, D))


def _fold(t, d):
    L, W = t.shape
    return t.reshape(L // d, d, W).transpose(1, 0, 2)


def _unfold(t):
    d, Lf, W = t.shape
    return t.transpose(1, 0, 2).reshape(Lf * d, W)


def _prep_in_weights(w_in):
    cols = lambda s: w_in[:, IN_OFFSETS[s]:IN_OFFSETS[s + 1]]
    w_rope = jnp.concatenate([cols(s) for s in ROPE_SLOTS], 1).astype(BF16)
    pad = jnp.zeros((w_in.shape[0], LANES - IN_SPLITS[SL_DV]), w_in.dtype)
    w_val = jnp.concatenate([cols(SL_BV), cols(SL_CV), cols(SL_AV), cols(SL_DV), pad], 1).astype(BF16)
    pad = jnp.zeros((w_in.shape[0], LANES - IN_SPLITS[SL_IW]), w_in.dtype)
    w_iw = jnp.concatenate([cols(SL_IW), pad], 1).astype(BF16)
    return w_rope, w_val, w_iw


def _mixer_layer(x, xin, tables, w_in, sinks, lam_p, sub_g, w_out, ln_g, ln_b, layer_idx, cfg):
    L = x.shape[0]
    w_rope, w_val, w_iw = _prep_in_weights(w_in)
    rp = _project(xin, w_rope, tables, BF16, cfg["proj_tm"], cfg["proj_tn"])
    vp = _project(xin, w_val, None, BF16, cfg["proj_tm"], VAL_WIDTH // 2)
    iw = _project(xin, w_iw, None, F32, cfg["proj_tm"], LANES)

    oa = _banded(rp[None], ROPE_OFF[SL_AQ] // 512, rp[None], ROPE_OFF[SL_AK] // LANES,
                 vp[None], VAL_OFF[SL_AV] // LANES, A_HEADS, A_KV_HEADS, SWA_WINDOW - 1,
                 sinks=sinks)[0]

    lam_init = 0.8 - 0.6 * math.exp(-0.3 * layer_idx)
    ob = _diff_attention(rp, vp, lam_p, sub_g, lam_init, cfg["diff_tq"], cfg["diff_tk"])

    cq = rp[:, ROPE_OFF[SL_CQ]:ROPE_OFF[SL_CQ] + 512]
    ck = rp[:, ROPE_OFF[SL_CK]:ROPE_OFF[SL_CK] + 512]
    cv = vp[:, VAL_OFF[SL_CV]:VAL_OFF[SL_CV] + 512]
    ocs, lses = [], []
    for window, dil in DILATION_PAIRS:
        o_c, lse_c = _banded(_fold(cq, dil), 0, _fold(ck, dil), 0, _fold(cv, dil), 0,
                             C_HEADS, C_HEADS, window // dil)
        ocs.append(_unfold(o_c))
        lses.append(_unfold(lse_c))

    dk = rp[:, ROPE_OFF[SL_DK]:ROPE_OFF[SL_DK] + HEAD_DIM]
    ik = rp[:, ROPE_OFF[SL_IK]:ROPE_OFF[SL_IK] + HEAD_DIM]
    dv = vp[:, VAL_OFF[SL_DV]:VAL_OFF[SL_DV] + LANES]
    dv = jnp.where(jnp.arange(LANES)[None, :] == HEAD_DIM, jnp.ones((), BF16), dv)
    od = _dsa_attention(rp, iw, dk, ik, dv, cfg["dsa_tk"])

    return _out_projection(x, oa, ob, ocs, lses, od, w_out.astype(BF16), ln_g, ln_b,
                           cfg["out_tm"])


def _peer_layer(x, xb, wq, keys, u, v, ln_g, ln_b, cfg):
    route = _peer_route(xb, wq.astype(BF16),
                        keys.reshape(2 * PEER_HEADS, N_KEYS, PEER_HALF).astype(BF16),
                        cfg["route_tm"])
    y = _peer_experts(xb, route, u.astype(BF16), v.astype(BF16), cfg["peer_tm"])
    return _add_ln(x, y, ln_g, ln_b, cfg["out_tm"])


def _config(L):
    return dict(proj_tm=min(512, L), proj_tn=768, diff_tq=min(512, L), diff_tk=min(1024, L), dsa_tk=min(1024, L),
                out_tm=min(256, L), route_tm=min(256, L), peer_tm=min(512, L))


def _trunk(x2, w_in, attn_sinks, diff_lambda, diff_norm_g, w_out, ln1_g, ln1_b,
           peer_wq, peer_keys, peer_u, peer_v, ln2_g, ln2_b):
    L = x2.shape[0]
    cfg = _config(L)
    tables = _rope_tables(L, cfg["proj_tn"])
    xin = x2
    for i in range(w_in.shape[0]):
        x2, xb = _mixer_layer(x2, xin, tables, w_in[i], attn_sinks[i], diff_lambda[i],
                              diff_norm_g[i], w_out[i], ln1_g[i], ln1_b[i], i, cfg)
        x2, xb = _peer_layer(x2, xb, peer_wq[i], peer_keys[i], peer_u[i], peer_v[i],
                             ln2_g[i], ln2_b[i], cfg)
        xin = xb
    return x2


def kernel(x, w_in, attn_sinks, diff_lambda, diff_norm_g, w_out, ln1_g, ln1_b,
           peer_wq, peer_keys, peer_u, peer_v, ln2_g, ln2_b):
    b, L, D = x.shape
    outs = [_trunk(x[i], w_in, attn_sinks, diff_lambda, diff_norm_g, w_out, ln1_g, ln1_b,
                   peer_wq, peer_keys, peer_u, peer_v, ln2_g, ln2_b) for i in range(b)]
    return jnp.stack(outs, 0)
```

```python
import functools
import math

import jax
import jax.numpy as jnp
import numpy as np
from jax import lax
from jax.experimental import pallas as pl
from jax.experimental.pallas import tpu as pltpu

F32 = jnp.float32
BF16 = jnp.bfloat16
I32 = jnp.int32

D_MODEL = 2048
DEPTH = 2
HEAD_DIM = 64
HALF_DIM = HEAD_DIM // 2
LANES = 128
BLOCK = 128
NEG_INF = -1e30
ROPE_THETA = 10000.0
LN_EPS = 1e-5
RMS_EPS = 1e-5
A_HEADS, A_KV_HEADS, SWA_WINDOW = 8, 2, 128
B_HEADS = 4
C_HEADS = 8
DILATION_PAIRS = ((128, 1), (512, 4), (2048, 16))
D_HEADS, IDX_HEADS, IDX_DIM = 8, 8, 64
DSA_TOPK_MAX = 256
PEER_HEADS, PEER_TOPK, N_KEYS = 8, 16, 128
PEER_QDIM = 256
PEER_HALF = PEER_QDIM // 2
ALPHA = (2 * DEPTH) ** 0.25
Q_SCALE = HEAD_DIM ** -0.5
INT_MIN = -2 ** 31

IN_SPLITS = (512, 128, 128, 512, 512, 512, 512, 512, 512, 512, 64, 64, 512, 64, 8)
IN_OFFSETS = tuple(int(v) for v in np.concatenate([[0], np.cumsum(IN_SPLITS)]))
(SL_AQ, SL_AK, SL_AV, SL_BQ, SL_BK, SL_BV, SL_CQ, SL_CK, SL_CV,
 SL_DQ, SL_DK, SL_DV, SL_IQ, SL_IK, SL_IW) = range(15)
ROPE_SLOTS = (SL_AQ, SL_BQ, SL_BK, SL_CQ, SL_CK, SL_DQ, SL_IQ, SL_AK, SL_DK, SL_IK)
ROPE_OFF = {}
_o = 0
for _s in ROPE_SLOTS:
    ROPE_OFF[_s] = _o
    _o += IN_SPLITS[_s]
ROPE_WIDTH = _o
VAL_OFF = {SL_BV: 0, SL_CV: 512, SL_AV: 1024, SL_DV: 1152}
VAL_WIDTH = 1280
VMEM_LIMIT = 56 * 1024 * 1024

NT_DIMS = (((1,), (1,)), ((), ()))
TN_DIMS = (((0,), (0,)), ((), ()))


def _cparams(*sem):
    return pltpu.CompilerParams(dimension_semantics=sem, vmem_limit_bytes=VMEM_LIMIT)


def _proj_kernel(x_ref, w_ref, *rest, rope):
    if rope:
        cos_ref, sa_ref, sb_ref, o_ref, xb_ref = rest
    else:
        o_ref, xb_ref = rest

    @pl.when(pl.program_id(1) == 0)
    def _():
        xb_ref[...] = x_ref[...].astype(BF16)

    y = jnp.dot(xb_ref[...], w_ref[...], preferred_element_type=F32)
    if rope:
        for t in range(y.shape[1] // LANES):
            sl = slice(t * LANES, (t + 1) * LANES)
            yt = y[:, sl]
            o_ref[:, sl] = (yt * cos_ref[:, sl]
                            + pltpu.roll(yt, LANES - HALF_DIM, 1) * sa_ref[:, sl]
                            + pltpu.roll(yt, HALF_DIM, 1) * sb_ref[:, sl]).astype(o_ref.dtype)
    else:
        o_ref[...] = y.astype(o_ref.dtype)


def _project(x, w, tables, out_dtype, tm, tn):
    L, K = x.shape
    N = w.shape[1]
    rope = tables is not None
    in_specs = [pl.BlockSpec((tm, K), lambda i, j: (i, 0)),
                pl.BlockSpec((K, tn), lambda i, j: (0, j))]
    args = [x, w]
    if rope:
        in_specs += [pl.BlockSpec((tm, tn), lambda i, j: (i, 0))] * 3
        args += list(tables)
    return pl.pallas_call(
        functools.partial(_proj_kernel, rope=rope),
        grid=(L // tm, N // tn),
        in_specs=in_specs,
        out_specs=pl.BlockSpec((tm, tn), lambda i, j: (i, j)),
        out_shape=jax.ShapeDtypeStruct((L, N), out_dtype),
        scratch_shapes=[pltpu.VMEM((tm, K), BF16)],
        compiler_params=_cparams("parallel", "arbitrary"),
        name="proj_rope" if rope else "proj_plain",
    )(*args)


def _rope_tables(L, width):
    inv_freq = ROPE_THETA ** (-jnp.arange(HALF_DIM, dtype=F32) / HALF_DIM)
    ang = jnp.arange(L, dtype=F32)[:, None] * inv_freq[None, :]
    cos, sin = jnp.cos(ang), jnp.sin(ang)
    zero = jnp.zeros_like(sin)
    reps = width // HEAD_DIM
    cos_t = jnp.tile(jnp.concatenate([cos, cos], 1), (1, reps))
    sin_a = jnp.tile(jnp.concatenate([-sin, zero], 1), (1, reps))
    sin_b = jnp.tile(jnp.concatenate([zero, sin], 1), (1, reps))
    return cos_t, sin_a, sin_b


def _banded_kernel(*refs, n_kv, group, max_dist, use_sink):
    if use_sink:
        sink_ref, q_ref, kp_ref, kc_ref, vp_ref, vc_ref, o_ref = refs
    else:
        q_ref, kp_ref, kc_ref, vp_ref, vc_ref, o_ref, lse_ref = refs
    n = pl.program_id(1)
    q = q_ref[...] * Q_SCALE
    k2 = jnp.concatenate([kp_ref[...], kc_ref[...]], 0)
    v2 = jnp.concatenate([vp_ref[...], vc_ref[...]], 0)
    row = lax.broadcasted_iota(I32, (BLOCK, 2 * BLOCK), 0)
    col = lax.broadcasted_iota(I32, (BLOCK, 2 * BLOCK), 1)
    dist = row + BLOCK - col
    lo = jnp.where(n > 0, 0, BLOCK)
    ok = (dist >= 0) & (dist <= max_dist) & (col >= lo)
    lane = lax.broadcasted_iota(I32, (BLOCK, LANES), 1)
    outs = []
    lse_tile = jnp.zeros((BLOCK, LANES), F32)
    for g in range(n_kv):
        kh = k2[:, g * HEAD_DIM:(g + 1) * HEAD_DIM]
        vh = v2[:, g * HEAD_DIM:(g + 1) * HEAD_DIM]
        for u in range(group):
            h = g * group + u
            qh = q[:, h * HEAD_DIM:(h + 1) * HEAD_DIM]
            s = lax.dot_general(qh, kh, NT_DIMS, preferred_element_type=F32)
            s = jnp.where(ok, s, NEG_INF)
            m = jnp.max(s, axis=1, keepdims=True)
            if use_sink:
                m = jnp.maximum(m, sink_ref[h])
            p = jnp.exp(s - m)
            l = jnp.sum(p, axis=1, keepdims=True)
            if use_sink:
                l = l + jnp.exp(sink_ref[h] - m)
            acc = jnp.dot(p.astype(BF16), vh, preferred_element_type=F32)
            outs.append(acc / l)
            if not use_sink:
                lse_tile = jnp.where(lane == h, m + jnp.log(l), lse_tile)
    o_ref[...] = jnp.concatenate(outs, 1).astype(o_ref.dtype)
    if not use_sink:
        lse_ref[...] = lse_tile


def _banded(q3, qcol, k3, kcol, v3, vcol, n_heads, n_kv, max_dist, sinks=None):
    d, Lf, _ = q3.shape
    nb = Lf // BLOCK
    wq, wk = n_heads * HEAD_DIM, n_kv * HEAD_DIM
    use_sink = sinks is not None
    cur = lambda c: (lambda r, n: (r, n, c))
    prev = lambda c: (lambda r, n: (r, jnp.maximum(n - 1, 0), c))
    in_specs = [pl.BlockSpec((None, BLOCK, wq), cur(qcol)),
                pl.BlockSpec((None, BLOCK, wk), prev(kcol)),
                pl.BlockSpec((None, BLOCK, wk), cur(kcol)),
                pl.BlockSpec((None, BLOCK, wk), prev(vcol)),
                pl.BlockSpec((None, BLOCK, wk), cur(vcol))]
    args = [q3, k3, k3, v3, v3]
    if use_sink:
        in_specs = [pl.BlockSpec(memory_space=pltpu.SMEM)] + in_specs
        args = [sinks] + args
        out_shape = jax.ShapeDtypeStruct((d, Lf, wq), BF16)
        out_specs = pl.BlockSpec((None, BLOCK, wq), cur(0))
    else:
        out_shape = (jax.ShapeDtypeStruct((d, Lf, wq), F32),
                     jax.ShapeDtypeStruct((d, Lf, LANES), F32))
        out_specs = (pl.BlockSpec((None, BLOCK, wq), cur(0)),
                     pl.BlockSpec((None, BLOCK, LANES), cur(0)))
    return pl.pallas_call(
        functools.partial(_banded_kernel, n_kv=n_kv, group=n_heads // n_kv,
                          max_dist=max_dist, use_sink=use_sink),
        grid=(d, nb),
        in_specs=in_specs,
        out_specs=out_specs,
        out_shape=out_shape,
        compiler_params=_cparams("parallel", "parallel"),
        name="swa_sink" if use_sink else "dilated_band",
    )(*args)


def _diff_kernel(lam_ref, g_ref, q_ref, k_ref, v_ref, o_ref, m_ref, acc_ref, *, tq, tk,
                 lam_init):
    i = pl.program_id(1)
    lane = lax.broadcasted_iota(I32, (tq, LANES), 1)
    q = q_ref[...] * Q_SCALE
    zero = jnp.zeros_like(q)
    qz = (jnp.where(lane < HEAD_DIM, q, zero), jnp.where(lane >= HEAD_DIM, q, zero))
    hq = tq // 2
    chains = [(s, r) for s in range(2) for r in range(2)]
    qc = [qz[s][r * hq:(r + 1) * hq] for s, r in chains]
    m_ref[...] = jnp.full(m_ref.shape, NEG_INF, F32)
    acc_ref[...] = jnp.zeros(acc_ref.shape, F32)
    ones_col = jnp.where(lax.broadcasted_iota(I32, (tk, LANES), 1) == 0, 1.0, 0.0).astype(BF16)

    def tile(j, masked):
        start = pl.multiple_of(j * tk, tk)
        kt = k_ref[pl.ds(start, tk), :]
        v_ext = jnp.concatenate([v_ref[pl.ds(start, tk), :], ones_col], 1)

        def scores(c):
            s = lax.dot_general(qc[c], kt, NT_DIMS, preferred_element_type=F32)
            if masked:
                qpos = i * tq + chains[c][1] * hq + lax.broadcasted_iota(I32, (hq, tk), 0)
                kpos = start + lax.broadcasted_iota(I32, (hq, tk), 1)
                s = jnp.where(kpos <= qpos, s, NEG_INF)
            return s

        s_next = scores(0)
        for c in range(len(chains)):
            s = s_next
            if c + 1 < len(chains):
                s_next = scores(c + 1)
            m_old = m_ref[c]
            m_new = jnp.maximum(m_old, jnp.max(s, axis=1, keepdims=True))
            p = jnp.exp(s - m_new).astype(BF16)
            acc_ref[c] = (jnp.exp(m_old - m_new) * acc_ref[c]
                          + jnp.dot(p, v_ext, preferred_element_type=F32))
            m_ref[c] = m_new

    def body(j, carry):
        tile(j, False)
        return carry

    n_full = (i * tq) // tk
    lax.fori_loop(0, n_full, body, 0)
    tile(n_full, True)

    lp = lam_ref[...]
    lam = (jnp.exp(jnp.sum(lp[0:1] * lp[1:2], axis=1, keepdims=True))
           - jnp.exp(jnp.sum(lp[2:3] * lp[3:4], axis=1, keepdims=True)) + lam_init)
    o = [acc_ref[c][:, :LANES] / acc_ref[c][:, LANES:LANES + 1] for c in range(len(chains))]
    ob = jnp.concatenate(o[0:2], 0) - lam * jnp.concatenate(o[2:4], 0)
    ob = ob * lax.rsqrt(jnp.mean(ob * ob, axis=1, keepdims=True) + RMS_EPS)
    o_ref[...] = (ob * g_ref[...] * (1.0 - lam_init)).astype(o_ref.dtype)


def _diff_attention(rp, vp, lam_p, sub_g, lam_init, tq, tk):
    L = rp.shape[0]
    qcol = ROPE_OFF[SL_BQ] // LANES
    kcol = ROPE_OFF[SL_BK] // LANES
    vcol = VAL_OFF[SL_BV] // LANES
    return pl.pallas_call(
        functools.partial(_diff_kernel, tq=tq, tk=tk, lam_init=lam_init),
        grid=(B_HEADS, L // tq),
        in_specs=[pl.BlockSpec((4, HEAD_DIM), lambda h, i: (0, 0)),
                  pl.BlockSpec((1, LANES), lambda h, i: (0, 0)),
                  pl.BlockSpec((tq, LANES), lambda h, i: (i, qcol + h)),
                  pl.BlockSpec((L, LANES), lambda h, i: (0, kcol + h)),
                  pl.BlockSpec((L, LANES), lambda h, i: (0, vcol + h))],
        out_specs=pl.BlockSpec((tq, LANES), lambda h, i: (i, h)),
        out_shape=jax.ShapeDtypeStruct((L, B_HEADS * LANES), BF16),
        scratch_shapes=[pltpu.VMEM((4, tq // 2, 1), F32),
                        pltpu.VMEM((4, tq // 2, 2 * LANES), F32)],
        compiler_params=_cparams("parallel", "arbitrary"),
        name="diff_attn",
    )(lam_p, sub_g.reshape(1, LANES), rp, rp, vp)


def _sortable(score):
    b = pltpu.bitcast(score, I32)
    return jnp.where(b >= 0, b, b ^ 0x7FFFFFFF)


DSA_PER = 2
DSA_CHAINS = D_HEADS // DSA_PER


def _dsa_kernel(q_ref, iq_ref, iw_ref, k_ref, ik_ref, v_ref, o_ref, key_ref, m_ref, acc_ref, *,
                tk, topk):
    n = pl.program_id(0)
    n_chunks = (n * BLOCK + BLOCK + tk - 1) // tk
    kpos = lax.broadcasted_iota(I32, (tk, BLOCK), 0)
    qpos = n * BLOCK + lax.broadcasted_iota(I32, (tk, BLOCK), 1)

    def chain_rows(t, g):
        return jnp.concatenate([t[:, h * HEAD_DIM:(h + 1) * HEAD_DIM]
                                for h in range(g * DSA_PER, (g + 1) * DSA_PER)], 0)

    def pipelined(first, rest):
        nxt = first(0)
        for g in range(DSA_CHAINS):
            cur = nxt
            if g + 1 < DSA_CHAINS:
                nxt = first(g + 1)
            rest(g, cur)

    iq_all = iq_ref[...] * (IDX_DIM ** -0.5)
    iqc = [chain_rows(iq_all, g) for g in range(DSA_CHAINS)]
    w_rows = (iw_ref[...] * (IDX_HEADS ** -0.5)).T

    def score_chunk(c, carry):
        start = pl.multiple_of(c * tk, tk)
        ikt = ik_ref[pl.ds(start, tk), :]
        parts = []

        def weighted(g, sc):
            part = None
            for u in range(DSA_PER):
                h = g * DSA_PER + u
                term = w_rows[h:h + 1] * jnp.maximum(sc[:, u * BLOCK:(u + 1) * BLOCK], 0.0)
                part = term if part is None else part + term
            parts.append(part)

        pipelined(lambda g: lax.dot_general(ikt, iqc[g], NT_DIMS, preferred_element_type=F32),
                  weighted)
        score = (parts[0] + parts[1]) + (parts[2] + parts[3])
        score = jnp.where(start + kpos <= qpos, score, NEG_INF)
        key_ref[c] = _sortable(score)
        return carry

    lax.fori_loop(0, n_chunks, score_chunk, 0)

    thr = jnp.full((1, BLOCK), INT_MIN, I32)
    for bit in range(31, -1, -1):
        cand = thr + np.int32(INT_MIN if bit == 31 else 1 << bit)

        def count_chunk(c, cnt, cand=cand):
            hit = jnp.where(key_ref[c] >= cand, 1, 0)
            return cnt + jnp.sum(hit.reshape(tk // 64, 64, BLOCK), axis=0)

        cnt = lax.fori_loop(0, n_chunks, count_chunk, jnp.zeros((64, BLOCK), I32))
        thr = jnp.where(jnp.sum(cnt, axis=0, keepdims=True) >= topk, cand, thr)

    q_all = q_ref[...] * Q_SCALE
    qc = [chain_rows(q_all, g) for g in range(DSA_CHAINS)]
    m_ref[...] = jnp.full(m_ref.shape, NEG_INF, F32)
    acc_ref[...] = jnp.zeros(acc_ref.shape, F32)

    def attend_chunk(c, carry):
        start = pl.multiple_of(c * tk, tk)
        keep = (key_ref[c] >= thr) & (start + kpos <= qpos)
        bias = jnp.where(keep, 0.0, NEG_INF).T[None]
        kt = k_ref[pl.ds(start, tk), :]
        vt = v_ref[pl.ds(start, tk), :]

        def softmax_pv(g, s):
            s = s.reshape(DSA_PER, BLOCK, tk) + bias
            m_old = m_ref[g]
            m_new = jnp.maximum(m_old, jnp.max(s, axis=2, keepdims=True))
            p = jnp.exp(s - jnp.maximum(m_new, 0.1 * NEG_INF))
            pv = jnp.dot(p.reshape(DSA_PER * BLOCK, tk).astype(BF16), vt,
                         preferred_element_type=F32)
            acc_ref[g] = (jnp.exp(m_old - m_new) * acc_ref[g]
                          + pv.reshape(DSA_PER, BLOCK, LANES))
            m_ref[g] = m_new

        pipelined(lambda g: lax.dot_general(qc[g], kt, NT_DIMS, preferred_element_type=F32),
                  softmax_pv)
        return carry

    lax.fori_loop(0, n_chunks, attend_chunk, 0)
    outs = []
    for g in range(DSA_CHAINS):
        acc = acc_ref[g]
        out = acc[:, :, :HEAD_DIM] / acc[:, :, HEAD_DIM:HEAD_DIM + 1]
        outs += [out[u] for u in range(DSA_PER)]
    o_ref[...] = jnp.concatenate(outs, 1).astype(o_ref.dtype)


def _dsa_attention(rp, iw, dk, ik, dv, tk):
    L = rp.shape[0]
    topk = min(DSA_TOPK_MAX, L // 4)
    wide = D_HEADS * HEAD_DIM
    return pl.pallas_call(
        functools.partial(_dsa_kernel, tk=tk, topk=topk),
        grid=(L // BLOCK,),
        in_specs=[pl.BlockSpec((BLOCK, wide), lambda n: (n, ROPE_OFF[SL_DQ] // wide)),
                  pl.BlockSpec((BLOCK, wide), lambda n: (n, ROPE_OFF[SL_IQ] // wide)),
                  pl.BlockSpec((BLOCK, LANES), lambda n: (n, 0)),
                  pl.BlockSpec((L, HEAD_DIM), lambda n: (0, 0)),
                  pl.BlockSpec((L, HEAD_DIM), lambda n: (0, 0)),
                  pl.BlockSpec((L, LANES), lambda n: (0, 0))],
        out_specs=pl.BlockSpec((BLOCK, wide), lambda n: (n, 0)),
        out_shape=jax.ShapeDtypeStruct((L, wide), BF16),
        scratch_shapes=[pltpu.VMEM((L // tk, tk, BLOCK), I32),
                        pltpu.VMEM((DSA_CHAINS, DSA_PER, BLOCK, 1), F32),
                        pltpu.VMEM((DSA_CHAINS, DSA_PER, BLOCK, LANES), F32)],
        compiler_params=_cparams("parallel"),
        name="dsa_attn",
    )(rp, rp, iw, dk, ik, dv)


def _layer_norm(z, g, b):
    mu = jnp.mean(z, axis=1, keepdims=True)
    zc = z - mu
    var = jnp.mean(zc * zc, axis=1, keepdims=True)
    return zc * lax.rsqrt(var + LN_EPS) * g + b


def _outproj_kernel(x_ref, oa_ref, ob_ref, oc1_ref, oc2_ref, oc3_ref, ls1_ref, ls2_ref, ls3_ref,
                    od_ref, w_ref, g_ref, b_ref, y_ref, yb_ref):
    ocs = (oc1_ref[...], oc2_ref[...], oc3_ref[...])
    lses = (ls1_ref[...], ls2_ref[...], ls3_ref[...])
    top = jnp.maximum(jnp.maximum(lses[0], lses[1]), lses[2])
    wts = [jnp.exp(t - top) for t in lses]
    inv = 1.0 / (wts[0] + wts[1] + wts[2])
    merged = []
    for h in range(C_HEADS):
        sl = slice(h * HEAD_DIM, (h + 1) * HEAD_DIM)
        num = sum(wts[c][:, h:h + 1] * ocs[c][:, sl] for c in range(3))
        merged.append(num * inv[:, h:h + 1])
    oc = jnp.concatenate(merged, 1).astype(BF16)
    mixed = jnp.concatenate([oa_ref[...], ob_ref[...], oc, od_ref[...]], 1)
    y = jnp.dot(mixed, w_ref[...], preferred_element_type=F32)
    out = _layer_norm(ALPHA * x_ref[...] + y, g_ref[...], b_ref[...])
    y_ref[...] = out
    yb_ref[...] = out.astype(BF16)


def _out_projection(x, oa, ob, ocs, lses, od, w, g, b, tm):
    L, D = x.shape
    gw = D // 4
    row = lambda i: (i, 0)
    fixed = lambda i: (0, 0)
    return pl.pallas_call(
        _outproj_kernel,
        grid=(L // tm,),
        in_specs=[pl.BlockSpec((tm, D), row), pl.BlockSpec((tm, gw), row),
                  pl.BlockSpec((tm, gw), row)]
                 + [pl.BlockSpec((tm, gw), row)] * 3 + [pl.BlockSpec((tm, LANES), row)] * 3
                 + [pl.BlockSpec((tm, gw), row), pl.BlockSpec((D, D), fixed),
                    pl.BlockSpec((1, D), fixed), pl.BlockSpec((1, D), fixed)],
        out_specs=(pl.BlockSpec((tm, D), row), pl.BlockSpec((tm, D), row)),
        out_shape=(jax.ShapeDtypeStruct((L, D), F32), jax.ShapeDtypeStruct((L, D), BF16)),
        compiler_params=_cparams("parallel"),
        name="outproj_ln",
    )(x, oa, ob, *ocs, *lses, od, w, g.reshape(1, D), b.reshape(1, D))


def _top_values(t, k, with_rank=False):
    vals = []
    rank = jnp.full(t.shape, float(k), F32)
    for i in range(k):
        mx = jnp.max(t, axis=0, keepdims=True)
        vals.append(mx)
        hit = t == mx
        if with_rank:
            rank = jnp.where(hit, float(i), rank)
        t = jnp.where(hit, -jnp.inf, t)
    return (vals, rank) if with_rank else vals


def _route_kernel(xb_ref, wq_ref, keys_ref, k1_ref, c1_ref, r2_ref, e2_ref):
    tm = xb_ref.shape[0]
    q = jnp.dot(xb_ref[...], wq_ref[...], preferred_element_type=F32).astype(BF16)
    for h in range(PEER_HEADS):
        st = []
        for p in range(2):
            lo = (2 * h + p) * PEER_HALF
            st.append(lax.dot_general(keys_ref[2 * h + p], q[:, lo:lo + PEER_HALF], NT_DIMS,
                                      preferred_element_type=F32))
        v1 = _top_values(st[0], PEER_TOPK)
        v2, rank2 = _top_values(st[1], PEER_TOPK, with_rank=True)
        v2_all = jnp.concatenate(v2, 0)
        row8 = lax.broadcasted_iota(I32, (8, tm), 0)
        cand = [v1[0] + v2_all]
        for i in range(1, 8):
            cand.append(jnp.where(row8 < PEER_TOPK // (i + 1), v1[i] + v2_all[:8], -jnp.inf))
        cand.append(jnp.concatenate(v1[8:], 0) + v2[0])
        mu = _top_values(jnp.concatenate(cand, 0), PEER_TOPK)
        z = sum(jnp.exp(m - mu[0]) for m in mu)
        tau = mu[PEER_TOPK - 1]
        k1 = jnp.zeros(st[0].shape, F32)
        for j in range(PEER_TOPK):
            k1 = k1 + jnp.where(st[0] + v2[j] >= tau, 1.0, 0.0)
        k1_ref[h] = k1.reshape(N_KEYS // 8, 8, tm)
        c1_ref[h] = (jnp.exp(st[0] - v1[0]) / z).reshape(N_KEYS // 8, 8, tm)
        r2_ref[h] = rank2.astype(BF16)
        e2_ref[h] = jnp.exp(st[1] - v2[0]).astype(BF16)


def _peer_route(xb, wq, keys, tm):
    L, D = xb.shape
    big = jax.ShapeDtypeStruct((PEER_HEADS, N_KEYS, L), BF16)
    big_spec = pl.BlockSpec((PEER_HEADS, N_KEYS, tm), lambda i: (0, 0, i))
    rows = jax.ShapeDtypeStruct((PEER_HEADS, N_KEYS // 8, 8, L), F32)
    rows_spec = pl.BlockSpec((PEER_HEADS, N_KEYS // 8, 8, tm), lambda i: (0, 0, 0, i))
    return pl.pallas_call(
        _route_kernel,
        grid=(L // tm,),
        in_specs=[pl.BlockSpec((tm, D), lambda i: (i, 0)),
                  pl.BlockSpec((D, PEER_HEADS * PEER_QDIM), lambda i: (0, 0)),
                  pl.BlockSpec((2 * PEER_HEADS, N_KEYS, PEER_HALF), lambda i: (0, 0, 0))],
        out_specs=(rows_spec, rows_spec, big_spec, big_spec),
        out_shape=(rows, rows, big, big),
        compiler_params=_cparams("parallel"),
        name="peer_route",
    )(xb, wq, keys)


PEER_ROWS = 8
PEER_TE = PEER_ROWS * N_KEYS


def _peer_kernel(xb_ref, k1_ref, c1_ref, r2_ref, e2_ref, u_ref, v_ref, y_ref, a_ref, h_ref,
                 ga_ref, gb_ref):
    j = pl.program_id(1)
    tm = xb_ref.shape[0]
    d_model = v_ref.shape[1]
    n_lane_tiles = tm // LANES
    pieces = [(r, c) for r in range(PEER_ROWS) for c in range(n_lane_tiles)]
    zero = jnp.zeros((), BF16)

    def gate_piece(g_ref, a_hi, r, c):
        ln = slice(c * LANES, (c + 1) * LANES)
        acc = jnp.zeros((N_KEYS, LANES), BF16)
        for h in range(PEER_HEADS):
            k1 = k1_ref[h, a_hi, r:r + 1, ln].astype(BF16)
            c1 = c1_ref[h, a_hi, r:r + 1, ln].astype(BF16)
            acc = acc + jnp.where(r2_ref[h, :, ln] < k1, e2_ref[h, :, ln] * c1, zero)
        g_ref[r * N_KEYS:(r + 1) * N_KEYS, ln] = acc

    @pl.when(j == 0)
    def _():
        y_ref[...] = jnp.zeros(y_ref.shape, F32)
        for r, c in pieces:
            gate_piece(ga_ref, 0, r, c)

    def step(g_cur, g_next):
        a_hi = jnp.minimum(j + 1, N_KEYS // PEER_ROWS - 1)
        todo = list(pieces)

        def gates(count):
            for _ in range(count):
                r, c = todo.pop(0)
                gate_piece(g_next, a_hi, r, c)

        n_split = 4
        per_dot = len(pieces) // (2 * n_split)
        kc = d_model // n_split
        h_t = None
        for q in range(n_split):
            part = lax.dot_general(u_ref[:, q * kc:(q + 1) * kc], xb_ref[:, q * kc:(q + 1) * kc],
                                   NT_DIMS, preferred_element_type=F32)
            h_t = part if h_t is None else part + h_t
            gates(per_dot)
        h_ref[...] = h_t
        for r, c in pieces:
            rows = slice(r * N_KEYS, (r + 1) * N_KEYS)
            ln = slice(c * LANES, (c + 1) * LANES)
            hh = h_ref[rows, ln]
            gelu = 0.5 * hh * (1.0 + lax.erf(hh * (2.0 ** -0.5)))
            a_ref[rows, ln] = g_cur[rows, ln] * gelu.astype(BF16)
        dc = d_model // n_split
        for q in range(n_split):
            cols = slice(q * dc, (q + 1) * dc)
            y_ref[:, cols] += lax.dot_general(a_ref[...], v_ref[:, cols], TN_DIMS,
                                              preferred_element_type=F32)
            gates(per_dot)
        gates(len(todo))

    @pl.when(j % 2 == 0)
    def _():
        step(ga_ref, gb_ref)

    @pl.when(j % 2 == 1)
    def _():
        step(gb_ref, ga_ref)


def _peer_experts(xb, route, u, v, tm):
    L, D = xb.shape
    th, c1, s2, e2 = route
    n_exp = u.shape[0]
    assert (n_exp // PEER_TE) % 2 == 0 and PEER_ROWS == 8
    big_spec = pl.BlockSpec((PEER_HEADS, N_KEYS, tm), lambda i, j: (0, 0, i))
    rows_spec = pl.BlockSpec((PEER_HEADS, N_KEYS // 8, 8, tm), lambda i, j: (0, 0, 0, i))
    return pl.pallas_call(
        _peer_kernel,
        grid=(L // tm, n_exp // PEER_TE),
        in_specs=[pl.BlockSpec((tm, D), lambda i, j: (i, 0)),
                  rows_spec, rows_spec, big_spec, big_spec,
                  pl.BlockSpec((PEER_TE, D), lambda i, j: (j, 0)),
                  pl.BlockSpec((PEER_TE, D), lambda i, j: (j, 0))],
        out_specs=pl.BlockSpec((tm, D), lambda i, j: (i, 0)),
        out_shape=jax.ShapeDtypeStruct((L, D), F32),
        scratch_shapes=[pltpu.VMEM((PEER_TE, tm), BF16), pltpu.VMEM((PEER_TE, tm), F32),
                        pltpu.VMEM((PEER_TE, tm), BF16), pltpu.VMEM((PEER_TE, tm), BF16)],
        compiler_params=_cparams("parallel", "arbitrary"),
        name="peer_experts",
    )(xb, th, c1, s2, e2, u, v)


def _add_ln_kernel(x_ref, y_ref, g_ref, b_ref, o_ref, ob_ref):
    out = _layer_norm(ALPHA * x_ref[...] + y_ref[...], g_ref[...], b_ref[...])
    o_ref[...] = out
    ob_ref[...] = out.astype(BF16)


def _add_ln(x, y, g, b, tm):
    L, D = x.shape
    row = lambda i: (i, 0)
    fixed = lambda i: (0, 0)
    return pl.pallas_call(
        _add_ln_kernel,
        grid=(L // tm,),
        in_specs=[pl.BlockSpec((tm, D), row), pl.BlockSpec((tm, D), row),
                  pl.BlockSpec((1, D), fixed), pl.BlockSpec((1, D), fixed)],
        out_specs=(pl.BlockSpec((tm, D), row), pl.BlockSpec((tm, D), row)),
        out_shape=(jax.ShapeDtypeStruct((L, D), F32), jax.ShapeDtypeStruct((L, D), BF16)),
        compiler_params=_cparams("parallel"),
        name="residual_ln",
    )(x, y, g.reshape(1, D), b.reshape(1, D))


def _fold(t, d):
    L, W = t.shape
    return t.reshape(L // d, d, W).transpose(1, 0, 2)


def _unfold(t):
    d, Lf, W = t.shape
    return t.transpose(1, 0, 2).reshape(Lf * d, W)


def _prep_in_weights(w_in):
    cols = lambda s: w_in[:, IN_OFFSETS[s]:IN_OFFSETS[s + 1]]
    w_rope = jnp.concatenate([cols(s) for s in ROPE_SLOTS], 1).astype(BF16)
    pad = jnp.zeros((w_in.shape[0], LANES - IN_SPLITS[SL_DV]), w_in.dtype)
    w_val = jnp.concatenate([cols(SL_BV), cols(SL_CV), cols(SL_AV), cols(SL_DV), pad], 1).astype(BF16)
    pad = jnp.zeros((w_in.shape[0], LANES - IN_SPLITS[SL_IW]), w_in.dtype)
    w_iw = jnp.concatenate([cols(SL_IW), pad], 1).astype(BF16)
    return w_rope, w_val, w_iw


def _mixer_layer(x, xin, tables, w_in, sinks, lam_p, sub_g, w_out, ln_g, ln_b, layer_idx, cfg):
    L = x.shape[0]
    w_rope, w_val, w_iw = _prep_in_weights(w_in)
    rp = _project(xin, w_rope, tables, BF16, cfg["proj_tm"], cfg["proj_tn"])
    vp = _project(xin, w_val, None, BF16, cfg["proj_tm"], VAL_WIDTH // 2)
    iw = _project(xin, w_iw, None, F32, cfg["proj_tm"], LANES)

    oa = _banded(rp[None], ROPE_OFF[SL_AQ] // 512, rp[None], ROPE_OFF[SL_AK] // LANES,
                 vp[None], VAL_OFF[SL_AV] // LANES, A_HEADS, A_KV_HEADS, SWA_WINDOW - 1,
                 sinks=sinks)[0]

    lam_init = 0.8 - 0.6 * math.exp(-0.3 * layer_idx)
    ob = _diff_attention(rp, vp, lam_p, sub_g, lam_init, cfg["diff_tq"], cfg["diff_tk"])

    cq = rp[:, ROPE_OFF[SL_CQ]:ROPE_OFF[SL_CQ] + 512]
    ck = rp[:, ROPE_OFF[SL_CK]:ROPE_OFF[SL_CK] + 512]
    cv = vp[:, VAL_OFF[SL_CV]:VAL_OFF[SL_CV] + 512]
    ocs, lses = [], []
    for window, dil in DILATION_PAIRS:
        o_c, lse_c = _banded(_fold(cq, dil), 0, _fold(ck, dil), 0, _fold(cv, dil), 0,
                             C_HEADS, C_HEADS, window // dil)
        ocs.append(_unfold(o_c))
        lses.append(_unfold(lse_c))

    dk = rp[:, ROPE_OFF[SL_DK]:ROPE_OFF[SL_DK] + HEAD_DIM]
    ik = rp[:, ROPE_OFF[SL_IK]:ROPE_OFF[SL_IK] + HEAD_DIM]
    dv = vp[:, VAL_OFF[SL_DV]:VAL_OFF[SL_DV] + LANES]
    dv = jnp.where(jnp.arange(LANES)[None, :] == HEAD_DIM, jnp.ones((), BF16), dv)
    od = _dsa_attention(rp, iw, dk, ik, dv, cfg["dsa_tk"])

    return _out_projection(x, oa, ob, ocs, lses, od, w_out.astype(BF16), ln_g, ln_b,
                           cfg["out_tm"])


def _peer_layer(x, xb, wq, keys, u, v, ln_g, ln_b, cfg):
    route = _peer_route(xb, wq.astype(BF16),
                        keys.reshape(2 * PEER_HEADS, N_KEYS, PEER_HALF).astype(BF16),
                        cfg["route_tm"])
    y = _peer_experts(xb, route, u.astype(BF16), v.astype(BF16), cfg["peer_tm"])
    return _add_ln(x, y, ln_g, ln_b, cfg["out_tm"])


def _config(L):
    return dict(proj_tm=min(512, L), proj_tn=768, diff_tq=min(512, L), diff_tk=min(1024, L), dsa_tk=min(1024, L),
                out_tm=min(256, L), route_tm=min(256, L), peer_tm=min(512, L))


def _trunk(x2, w_in, attn_sinks, diff_lambda, diff_norm_g, w_out, ln1_g, ln1_b,
           peer_wq, peer_keys, peer_u, peer_v, ln2_g, ln2_b):
    L = x2.shape[0]
    cfg = _config(L)
    tables = _rope_tables(L, cfg["proj_tn"])
    xin = x2
    for i in range(w_in.shape[0]):
        x2, xb = _mixer_layer(x2, xin, tables, w_in[i], attn_sinks[i], diff_lambda[i],
                              diff_norm_g[i], w_out[i], ln1_g[i], ln1_b[i], i, cfg)
        x2, xb = _peer_layer(x2, xb, peer_wq[i], peer_keys[i], peer_u[i], peer_v[i],
                             ln2_g[i], ln2_b[i], cfg)
        xin = xb
    return x2


def kernel(x, w_in, attn_sinks, diff_lambda, diff_norm_g, w_out, ln1_g, ln1_b,
           peer_wq, peer_keys, peer_u, peer_v, ln2_g, ln2_b):
    b, L, D = x.shape
    outs = [_trunk(x[i], w_in, attn_sinks, diff_lambda, diff_norm_g, w_out, ln1_g, ln1_b,
                   peer_wq, peer_keys, peer_u, peer_v, ln2_g, ln2_b) for i in range(b)]
    return jnp.stack(outs, 0)
```

```python
import functools
import math

import jax
import jax.numpy as jnp
import numpy as np
from jax import lax
from jax.experimental import pallas as pl
from jax.experimental.pallas import tpu as pltpu

F32 = jnp.float32
BF16 = jnp.bfloat16
I32 = jnp.int32

D_MODEL = 2048
DEPTH = 2
HEAD_DIM = 64
HALF_DIM = HEAD_DIM // 2
LANES = 128
BLOCK = 128
NEG_INF = -1e30
ROPE_THETA = 10000.0
LN_EPS = 1e-5
RMS_EPS = 1e-5
A_HEADS, A_KV_HEADS, SWA_WINDOW = 8, 2, 128
B_HEADS = 4
C_HEADS = 8
DILATION_PAIRS = ((128, 1), (512, 4), (2048, 16))
D_HEADS, IDX_HEADS, IDX_DIM = 8, 8, 64
DSA_TOPK_MAX = 256
PEER_HEADS, PEER_TOPK, N_KEYS = 8, 16, 128
PEER_QDIM = 256
PEER_HALF = PEER_QDIM // 2
ALPHA = (2 * DEPTH) ** 0.25
Q_SCALE = HEAD_DIM ** -0.5
INT_MIN = -2 ** 31

IN_SPLITS = (512, 128, 128, 512, 512, 512, 512, 512, 512, 512, 64, 64, 512, 64, 8)
IN_OFFSETS = tuple(int(v) for v in np.concatenate([[0], np.cumsum(IN_SPLITS)]))
(SL_AQ, SL_AK, SL_AV, SL_BQ, SL_BK, SL_BV, SL_CQ, SL_CK, SL_CV,
 SL_DQ, SL_DK, SL_DV, SL_IQ, SL_IK, SL_IW) = range(15)
ROPE_SLOTS = (SL_AQ, SL_BQ, SL_BK, SL_CQ, SL_CK, SL_DQ, SL_IQ, SL_AK, SL_DK, SL_IK)
ROPE_OFF = {}
_o = 0
for _s in ROPE_SLOTS:
    ROPE_OFF[_s] = _o
    _o += IN_SPLITS[_s]
ROPE_WIDTH = _o
VAL_OFF = {SL_BV: 0, SL_CV: 512, SL_AV: 1024, SL_DV: 1152}
VAL_WIDTH = 1280
VMEM_LIMIT = 56 * 1024 * 1024

NT_DIMS = (((1,), (1,)), ((), ()))
TN_DIMS = (((0,), (0,)), ((), ()))


def _cparams(*sem):
    return pltpu.CompilerParams(dimension_semantics=sem, vmem_limit_bytes=VMEM_LIMIT)


def _proj_kernel(x_ref, w_ref, *rest, rope):
    if rope:
        cos_ref, sa_ref, sb_ref, o_ref, xb_ref = rest
    else:
        o_ref, xb_ref = rest

    @pl.when(pl.program_id(1) == 0)
    def _():
        xb_ref[...] = x_ref[...].astype(BF16)

    y = jnp.dot(xb_ref[...], w_ref[...], preferred_element_type=F32)
    if rope:
        for t in range(y.shape[1] // LANES):
            sl = slice(t * LANES, (t + 1) * LANES)
            yt = y[:, sl]
            o_ref[:, sl] = (yt * cos_ref[:, sl]
                            + pltpu.roll(yt, LANES - HALF_DIM, 1) * sa_ref[:, sl]
                            + pltpu.roll(yt, HALF_DIM, 1) * sb_ref[:, sl]).astype(o_ref.dtype)
    else:
        o_ref[...] = y.astype(o_ref.dtype)


def _project(x, w, tables, out_dtype, tm, tn):
    L, K = x.shape
    N = w.shape[1]
    rope = tables is not None
    in_specs = [pl.BlockSpec((tm, K), lambda i, j: (i, 0)),
                pl.BlockSpec((K, tn), lambda i, j: (0, j))]
    args = [x, w]
    if rope:
        in_specs += [pl.BlockSpec((tm, tn), lambda i, j: (i, 0))] * 3
        args += list(tables)
    return pl.pallas_call(
        functools.partial(_proj_kernel, rope=rope),
        grid=(L // tm, N // tn),
        in_specs=in_specs,
        out_specs=pl.BlockSpec((tm, tn), lambda i, j: (i, j)),
        out_shape=jax.ShapeDtypeStruct((L, N), out_dtype),
        scratch_shapes=[pltpu.VMEM((tm, K), BF16)],
        compiler_params=_cparams("parallel", "arbitrary"),
        name="proj_rope" if rope else "proj_plain",
    )(*args)


def _rope_tables(L, width):
    inv_freq = ROPE_THETA ** (-jnp.arange(HALF_DIM, dtype=F32) / HALF_DIM)
    ang = jnp.arange(L, dtype=F32)[:, None] * inv_freq[None, :]
    cos, sin = jnp.cos(ang), jnp.sin(ang)
    zero = jnp.zeros_like(sin)
    reps = width // HEAD_DIM
    cos_t = jnp.tile(jnp.concatenate([cos, cos], 1), (1, reps))
    sin_a = jnp.tile(jnp.concatenate([-sin, zero], 1), (1, reps))
    sin_b = jnp.tile(jnp.concatenate([zero, sin], 1), (1, reps))
    return cos_t, sin_a, sin_b


def _banded_kernel(*refs, n_kv, group, max_dist, use_sink):
    if use_sink:
        sink_ref, q_ref, kp_ref, kc_ref, vp_ref, vc_ref, o_ref = refs
    else:
        q_ref, kp_ref, kc_ref, vp_ref, vc_ref, o_ref, lse_ref = refs
    n = pl.program_id(1)
    q = q_ref[...] * Q_SCALE
    k2 = jnp.concatenate([kp_ref[...], kc_ref[...]], 0)
    v2 = jnp.concatenate([vp_ref[...], vc_ref[...]], 0)
    row = lax.broadcasted_iota(I32, (BLOCK, 2 * BLOCK), 0)
    col = lax.broadcasted_iota(I32, (BLOCK, 2 * BLOCK), 1)
    dist = row + BLOCK - col
    lo = jnp.where(n > 0, 0, BLOCK)
    ok = (dist >= 0) & (dist <= max_dist) & (col >= lo)
    lane = lax.broadcasted_iota(I32, (BLOCK, LANES), 1)
    outs = []
    lse_tile = jnp.zeros((BLOCK, LANES), F32)
    for g in range(n_kv):
        kh = k2[:, g * HEAD_DIM:(g + 1) * HEAD_DIM]
        vh = v2[:, g * HEAD_DIM:(g + 1) * HEAD_DIM]
        for u in range(group):
            h = g * group + u
            qh = q[:, h * HEAD_DIM:(h + 1) * HEAD_DIM]
            s = lax.dot_general(qh, kh, NT_DIMS, preferred_element_type=F32)
            s = jnp.where(ok, s, NEG_INF)
            m = jnp.max(s, axis=1, keepdims=True)
            if use_sink:
                m = jnp.maximum(m, sink_ref[h])
            p = jnp.exp(s - m)
            l = jnp.sum(p, axis=1, keepdims=True)
            if use_sink:
                l = l + jnp.exp(sink_ref[h] - m)
            acc = jnp.dot(p.astype(BF16), vh, preferred_element_type=F32)
            outs.append(acc / l)
            if not use_sink:
                lse_tile = jnp.where(lane == h, m + jnp.log(l), lse_tile)
    o_ref[...] = jnp.concatenate(outs, 1).astype(o_ref.dtype)
    if not use_sink:
        lse_ref[...] = lse_tile


def _banded(q3, qcol, k3, kcol, v3, vcol, n_heads, n_kv, max_dist, sinks=None):
    d, Lf, _ = q3.shape
    nb = Lf // BLOCK
    wq, wk = n_heads * HEAD_DIM, n_kv * HEAD_DIM
    use_sink = sinks is not None
    cur = lambda c: (lambda r, n: (r, n, c))
    prev = lambda c: (lambda r, n: (r, jnp.maximum(n - 1, 0), c))
    in_specs = [pl.BlockSpec((None, BLOCK, wq), cur(qcol)),
                pl.BlockSpec((None, BLOCK, wk), prev(kcol)),
                pl.BlockSpec((None, BLOCK, wk), cur(kcol)),
                pl.BlockSpec((None, BLOCK, wk), prev(vcol)),
                pl.BlockSpec((None, BLOCK, wk), cur(vcol))]
    args = [q3, k3, k3, v3, v3]
    if use_sink:
        in_specs = [pl.BlockSpec(memory_space=pltpu.SMEM)] + in_specs
        args = [sinks] + args
        out_shape = jax.ShapeDtypeStruct((d, Lf, wq), BF16)
        out_specs = pl.BlockSpec((None, BLOCK, wq), cur(0))
    else:
        out_shape = (jax.ShapeDtypeStruct((d, Lf, wq), F32),
                     jax.ShapeDtypeStruct((d, Lf, LANES), F32))
        out_specs = (pl.BlockSpec((None, BLOCK, wq), cur(0)),
                     pl.BlockSpec((None, BLOCK, LANES), cur(0)))
    return pl.pallas_call(
        functools.partial(_banded_kernel, n_kv=n_kv, group=n_heads // n_kv,
                          max_dist=max_dist, use_sink=use_sink),
        grid=(d, nb),
        in_specs=in_specs,
        out_specs=out_specs,
        out_shape=out_shape,
        compiler_params=_cparams("parallel", "parallel"),
        name="swa_sink" if use_sink else "dilated_band",
    )(*args)


def _diff_kernel(lam_ref, g_ref, q_ref, k_ref, v_ref, o_ref, m_ref, acc_ref, *, tq, tk,
                 lam_init):
    i = pl.program_id(1)
    lane = lax.broadcasted_iota(I32, (tq, LANES), 1)
    q = q_ref[...] * Q_SCALE
    zero = jnp.zeros_like(q)
    qz = (jnp.where(lane < HEAD_DIM, q, zero), jnp.where(lane >= HEAD_DIM, q, zero))
    hq = tq // 2
    chains = [(s, r) for s in range(2) for r in range(2)]
    qc = [qz[s][r * hq:(r + 1) * hq] for s, r in chains]
    m_ref[...] = jnp.full(m_ref.shape, NEG_INF, F32)
    acc_ref[...] = jnp.zeros(acc_ref.shape, F32)
    ones_col = jnp.where(lax.broadcasted_iota(I32, (tk, LANES), 1) == 0, 1.0, 0.0).astype(BF16)

    def tile(j, masked):
        start = pl.multiple_of(j * tk, tk)
        kt = k_ref[pl.ds(start, tk), :]
        v_ext = jnp.concatenate([v_ref[pl.ds(start, tk), :], ones_col], 1)

        def scores(c):
            s = lax.dot_general(qc[c], kt, NT_DIMS, preferred_element_type=F32)
            if masked:
                qpos = i * tq + chains[c][1] * hq + lax.broadcasted_iota(I32, (hq, tk), 0)
                kpos = start + lax.broadcasted_iota(I32, (hq, tk), 1)
                s = jnp.where(kpos <= qpos, s, NEG_INF)
            return s

        s_next = scores(0)
        for c in range(len(chains)):
            s = s_next
            if c + 1 < len(chains):
                s_next = scores(c + 1)
            m_old = m_ref[c]
            m_new = jnp.maximum(m_old, jnp.max(s, axis=1, keepdims=True))
            p = jnp.exp(s - m_new).astype(BF16)
            acc_ref[c] = (jnp.exp(m_old - m_new) * acc_ref[c]
                          + jnp.dot(p, v_ext, preferred_element_type=F32))
            m_ref[c] = m_new

    def body(j, carry):
        tile(j, False)
        return carry

    n_full = (i * tq) // tk
    lax.fori_loop(0, n_full, body, 0)
    tile(n_full, True)

    lp = lam_ref[...]
    lam = (jnp.exp(jnp.sum(lp[0:1] * lp[1:2], axis=1, keepdims=True))
           - jnp.exp(jnp.sum(lp[2:3] * lp[3:4], axis=1, keepdims=True)) + lam_init)
    o = [acc_ref[c][:, :LANES] / acc_ref[c][:, LANES:LANES + 1] for c in range(len(chains))]
    ob = jnp.concatenate(o[0:2], 0) - lam * jnp.concatenate(o[2:4], 0)
    ob = ob * lax.rsqrt(jnp.mean(ob * ob, axis=1, keepdims=True) + RMS_EPS)
    o_ref[...] = (ob * g_ref[...] * (1.0 - lam_init)).astype(o_ref.dtype)


def _diff_attention(rp, vp, lam_p, sub_g, lam_init, tq, tk):
    L = rp.shape[0]
    qcol = ROPE_OFF[SL_BQ] // LANES
    kcol = ROPE_OFF[SL_BK] // LANES
    vcol = VAL_OFF[SL_BV] // LANES
    return pl.pallas_call(
        functools.partial(_diff_kernel, tq=tq, tk=tk, lam_init=lam_init),
        grid=(B_HEADS, L // tq),
        in_specs=[pl.BlockSpec((4, HEAD_DIM), lambda h, i: (0, 0)),
                  pl.BlockSpec((1, LANES), lambda h, i: (0, 0)),
                  pl.BlockSpec((tq, LANES), lambda h, i: (i, qcol + h)),
                  pl.BlockSpec((L, LANES), lambda h, i: (0, kcol + h)),
                  pl.BlockSpec((L, LANES), lambda h, i: (0, vcol + h))],
        out_specs=pl.BlockSpec((tq, LANES), lambda h, i: (i, h)),
        out_shape=jax.ShapeDtypeStruct((L, B_HEADS * LANES), BF16),
        scratch_shapes=[pltpu.VMEM((4, tq // 2, 1), F32),
                        pltpu.VMEM((4, tq // 2, 2 * LANES), F32)],
        compiler_params=_cparams("parallel", "arbitrary"),
        name="diff_attn",
    )(lam_p, sub_g.reshape(1, LANES), rp, rp, vp)


def _sortable(score):
    b = pltpu.bitcast(score, I32)
    return jnp.where(b >= 0, b, b ^ 0x7FFFFFFF)


DSA_PER = 2
DSA_CHAINS = D_HEADS // DSA_PER


def _dsa_kernel(q_ref, iq_ref, iw_ref, k_ref, ik_ref, v_ref, o_ref, key_ref, m_ref, acc_ref, *,
                tk, topk):
    n = pl.program_id(0)
    n_chunks = (n * BLOCK + BLOCK + tk - 1) // tk
    kpos = lax.broadcasted_iota(I32, (tk, BLOCK), 0)
    qpos = n * BLOCK + lax.broadcasted_iota(I32, (tk, BLOCK), 1)

    def chain_rows(t, g):
        return jnp.concatenate([t[:, h * HEAD_DIM:(h + 1) * HEAD_DIM]
                                for h in range(g * DSA_PER, (g + 1) * DSA_PER)], 0)

    def pipelined(first, rest):
        nxt = first(0)
        for g in range(DSA_CHAINS):
            cur = nxt
            if g + 1 < DSA_CHAINS:
                nxt = first(g + 1)
            rest(g, cur)

    iq_all = iq_ref[...] * (IDX_DIM ** -0.5)
    iqc = [chain_rows(iq_all, g) for g in range(DSA_CHAINS)]
    w_rows = (iw_ref[...] * (IDX_HEADS ** -0.5)).T

    def score_chunk(c, carry):
        start = pl.multiple_of(c * tk, tk)
        ikt = ik_ref[pl.ds(start, tk), :]
        parts = []

        def weighted(g, sc):
            part = None
            for u in range(DSA_PER):
                h = g * DSA_PER + u
                term = w_rows[h:h + 1] * jnp.maximum(sc[:, u * BLOCK:(u + 1) * BLOCK], 0.0)
                part = term if part is None else part + term
            parts.append(part)

        pipelined(lambda g: lax.dot_general(ikt, iqc[g], NT_DIMS, preferred_element_type=F32),
                  weighted)
        score = (parts[0] + parts[1]) + (parts[2] + parts[3])
        score = jnp.where(start + kpos <= qpos, score, NEG_INF)
        key_ref[c] = _sortable(score)
        return carry

    lax.fori_loop(0, n_chunks, score_chunk, 0)

    def search_bits(bits, state):
        thr, kept = state
        for bit in bits:
            cand = thr + np.int32(INT_MIN if bit == 31 else 1 << bit)

            def count_chunk(c, cnt, cand=cand):
                hit = jnp.where(key_ref[c] >= cand, 1, 0)
                return cnt + jnp.sum(hit.reshape(tk // 64, 64, BLOCK), axis=0)

            cnt = lax.fori_loop(0, n_chunks, count_chunk, jnp.zeros((64, BLOCK), I32))
            total = jnp.sum(cnt, axis=0, keepdims=True)
            ok = total >= topk
            thr = jnp.where(ok, cand, thr)
            kept = jnp.where(ok, total, kept)
        return thr, kept

    state = (jnp.full((1, BLOCK), INT_MIN, I32), jnp.full((1, BLOCK), 2 ** 30, I32))
    state = search_bits(range(31, 15, -1), state)
    for hi in (15, 11, 7, 3):
        settled = jnp.min(jnp.where(state[1] == topk, 1, 0)) == 1
        state = lax.cond(settled, lambda s: s,
                         functools.partial(search_bits, range(hi, hi - 4, -1)), state)
    thr = state[0]

    q_all = q_ref[...] * Q_SCALE
    qc = [chain_rows(q_all, g) for g in range(DSA_CHAINS)]
    m_ref[...] = jnp.full(m_ref.shape, NEG_INF, F32)
    acc_ref[...] = jnp.zeros(acc_ref.shape, F32)

    def attend_chunk(c, carry):
        start = pl.multiple_of(c * tk, tk)
        keep = (key_ref[c] >= thr) & (start + kpos <= qpos)
        bias = jnp.where(keep, 0.0, NEG_INF).T[None]
        kt = k_ref[pl.ds(start, tk), :]
        vt = v_ref[pl.ds(start, tk), :]

        def softmax_pv(g, s):
            s = s.reshape(DSA_PER, BLOCK, tk) + bias
            m_old = m_ref[g]
            m_new = jnp.maximum(m_old, jnp.max(s, axis=2, keepdims=True))
            p = jnp.exp(s - jnp.maximum(m_new, 0.1 * NEG_INF))
            pv = jnp.dot(p.reshape(DSA_PER * BLOCK, tk).astype(BF16), vt,
                         preferred_element_type=F32)
            acc_ref[g] = (jnp.exp(m_old - m_new) * acc_ref[g]
                          + pv.reshape(DSA_PER, BLOCK, LANES))
            m_ref[g] = m_new

        pipelined(lambda g: lax.dot_general(qc[g], kt, NT_DIMS, preferred_element_type=F32),
                  softmax_pv)
        return carry

    lax.fori_loop(0, n_chunks, attend_chunk, 0)
    outs = []
    for g in range(DSA_CHAINS):
        acc = acc_ref[g]
        out = acc[:, :, :HEAD_DIM] / acc[:, :, HEAD_DIM:HEAD_DIM + 1]
        outs += [out[u] for u in range(DSA_PER)]
    o_ref[...] = jnp.concatenate(outs, 1).astype(o_ref.dtype)


def _dsa_attention(rp, iw, dk, ik, dv, tk):
    L = rp.shape[0]
    topk = min(DSA_TOPK_MAX, L // 4)
    wide = D_HEADS * HEAD_DIM
    return pl.pallas_call(
        functools.partial(_dsa_kernel, tk=tk, topk=topk),
        grid=(L // BLOCK,),
        in_specs=[pl.BlockSpec((BLOCK, wide), lambda n: (n, ROPE_OFF[SL_DQ] // wide)),
                  pl.BlockSpec((BLOCK, wide), lambda n: (n, ROPE_OFF[SL_IQ] // wide)),
                  pl.BlockSpec((BLOCK, LANES), lambda n: (n, 0)),
                  pl.BlockSpec((L, HEAD_DIM), lambda n: (0, 0)),
                  pl.BlockSpec((L, HEAD_DIM), lambda n: (0, 0)),
                  pl.BlockSpec((L, LANES), lambda n: (0, 0))],
        out_specs=pl.BlockSpec((BLOCK, wide), lambda n: (n, 0)),
        out_shape=jax.ShapeDtypeStruct((L, wide), BF16),
        scratch_shapes=[pltpu.VMEM((L // tk, tk, BLOCK), I32),
                        pltpu.VMEM((DSA_CHAINS, DSA_PER, BLOCK, 1), F32),
                        pltpu.VMEM((DSA_CHAINS, DSA_PER, BLOCK, LANES), F32)],
        compiler_params=_cparams("parallel"),
        name="dsa_attn",
    )(rp, rp, iw, dk, ik, dv)


def _layer_norm(z, g, b):
    mu = jnp.mean(z, axis=1, keepdims=True)
    zc = z - mu
    var = jnp.mean(zc * zc, axis=1, keepdims=True)
    return zc * lax.rsqrt(var + LN_EPS) * g + b


def _outproj_kernel(x_ref, oa_ref, ob_ref, oc1_ref, oc2_ref, oc3_ref, ls1_ref, ls2_ref, ls3_ref,
                    od_ref, w_ref, g_ref, b_ref, y_ref, yb_ref):
    ocs = (oc1_ref[...], oc2_ref[...], oc3_ref[...])
    lses = (ls1_ref[...], ls2_ref[...], ls3_ref[...])
    top = jnp.maximum(jnp.maximum(lses[0], lses[1]), lses[2])
    wts = [jnp.exp(t - top) for t in lses]
    inv = 1.0 / (wts[0] + wts[1] + wts[2])
    merged = []
    for h in range(C_HEADS):
        sl = slice(h * HEAD_DIM, (h + 1) * HEAD_DIM)
        num = sum(wts[c][:, h:h + 1] * ocs[c][:, sl] for c in range(3))
        merged.append(num * inv[:, h:h + 1])
    oc = jnp.concatenate(merged, 1).astype(BF16)
    mixed = jnp.concatenate([oa_ref[...], ob_ref[...], oc, od_ref[...]], 1)
    y = jnp.dot(mixed, w_ref[...], preferred_element_type=F32)
    out = _layer_norm(ALPHA * x_ref[...] + y, g_ref[...], b_ref[...])
    y_ref[...] = out
    yb_ref[...] = out.astype(BF16)


def _out_projection(x, oa, ob, ocs, lses, od, w, g, b, tm):
    L, D = x.shape
    gw = D // 4
    row = lambda i: (i, 0)
    fixed = lambda i: (0, 0)
    return pl.pallas_call(
        _outproj_kernel,
        grid=(L // tm,),
        in_specs=[pl.BlockSpec((tm, D), row), pl.BlockSpec((tm, gw), row),
                  pl.BlockSpec((tm, gw), row)]
                 + [pl.BlockSpec((tm, gw), row)] * 3 + [pl.BlockSpec((tm, LANES), row)] * 3
                 + [pl.BlockSpec((tm, gw), row), pl.BlockSpec((D, D), fixed),
                    pl.BlockSpec((1, D), fixed), pl.BlockSpec((1, D), fixed)],
        out_specs=(pl.BlockSpec((tm, D), row), pl.BlockSpec((tm, D), row)),
        out_shape=(jax.ShapeDtypeStruct((L, D), F32), jax.ShapeDtypeStruct((L, D), BF16)),
        compiler_params=_cparams("parallel"),
        name="outproj_ln",
    )(x, oa, ob, *ocs, *lses, od, w, g.reshape(1, D), b.reshape(1, D))


def _top_values(t, k, with_rank=False):
    vals = []
    rank = jnp.full(t.shape, float(k), F32)
    for i in range(k):
        mx = jnp.max(t, axis=0, keepdims=True)
        vals.append(mx)
        hit = t == mx
        if with_rank:
            rank = jnp.where(hit, float(i), rank)
        t = jnp.where(hit, -jnp.inf, t)
    return (vals, rank) if with_rank else vals


def _route_kernel(xb_ref, wq_ref, keys_ref, k1_ref, c1_ref, r2_ref, e2_ref):
    tm = xb_ref.shape[0]
    q = jnp.dot(xb_ref[...], wq_ref[...], preferred_element_type=F32).astype(BF16)
    for h in range(PEER_HEADS):
        st = []
        for p in range(2):
            lo = (2 * h + p) * PEER_HALF
            st.append(lax.dot_general(keys_ref[2 * h + p], q[:, lo:lo + PEER_HALF], NT_DIMS,
                                      preferred_element_type=F32))
        v1 = _top_values(st[0], PEER_TOPK)
        v2, rank2 = _top_values(st[1], PEER_TOPK, with_rank=True)
        v2_all = jnp.concatenate(v2, 0)
        row8 = lax.broadcasted_iota(I32, (8, tm), 0)
        cand = [v1[0] + v2_all]
        for i in range(1, 8):
            cand.append(jnp.where(row8 < PEER_TOPK // (i + 1), v1[i] + v2_all[:8], -jnp.inf))
        cand.append(jnp.concatenate(v1[8:], 0) + v2[0])
        mu = _top_values(jnp.concatenate(cand, 0), PEER_TOPK)
        z = sum(jnp.exp(m - mu[0]) for m in mu)
        tau = mu[PEER_TOPK - 1]
        k1 = jnp.zeros(st[0].shape, F32)
        for j in range(PEER_TOPK):
            k1 = k1 + jnp.where(st[0] + v2[j] >= tau, 1.0, 0.0)
        k1_ref[h] = k1.reshape(N_KEYS // 8, 8, tm)
        c1_ref[h] = (jnp.exp(st[0] - v1[0]) / z).reshape(N_KEYS // 8, 8, tm)
        r2_ref[h] = rank2.astype(BF16)
        e2_ref[h] = jnp.exp(st[1] - v2[0]).astype(BF16)


def _peer_route(xb, wq, keys, tm):
    L, D = xb.shape
    big = jax.ShapeDtypeStruct((PEER_HEADS, N_KEYS, L), BF16)
    big_spec = pl.BlockSpec((PEER_HEADS, N_KEYS, tm), lambda i: (0, 0, i))
    rows = jax.ShapeDtypeStruct((PEER_HEADS, N_KEYS // 8, 8, L), F32)
    rows_spec = pl.BlockSpec((PEER_HEADS, N_KEYS // 8, 8, tm), lambda i: (0, 0, 0, i))
    return pl.pallas_call(
        _route_kernel,
        grid=(L // tm,),
        in_specs=[pl.BlockSpec((tm, D), lambda i: (i, 0)),
                  pl.BlockSpec((D, PEER_HEADS * PEER_QDIM), lambda i: (0, 0)),
                  pl.BlockSpec((2 * PEER_HEADS, N_KEYS, PEER_HALF), lambda i: (0, 0, 0))],
        out_specs=(rows_spec, rows_spec, big_spec, big_spec),
        out_shape=(rows, rows, big, big),
        compiler_params=_cparams("parallel"),
        name="peer_route",
    )(xb, wq, keys)


PEER_ROWS = 8
PEER_TE = PEER_ROWS * N_KEYS


def _peer_kernel(xb_ref, k1_ref, c1_ref, r2_ref, e2_ref, u_ref, v_ref, y_ref, a_ref, h_ref,
                 ga_ref, gb_ref):
    j = pl.program_id(1)
    tm = xb_ref.shape[0]
    d_model = v_ref.shape[1]
    n_lane_tiles = tm // LANES
    pieces = [(r, c) for r in range(PEER_ROWS) for c in range(n_lane_tiles)]
    zero = jnp.zeros((), BF16)

    def gate_piece(g_ref, a_hi, r, c):
        ln = slice(c * LANES, (c + 1) * LANES)
        acc = jnp.zeros((N_KEYS, LANES), BF16)
        for h in range(PEER_HEADS):
            k1 = k1_ref[h, a_hi, r:r + 1, ln].astype(BF16)
            c1 = c1_ref[h, a_hi, r:r + 1, ln].astype(BF16)
            acc = acc + jnp.where(r2_ref[h, :, ln] < k1, e2_ref[h, :, ln] * c1, zero)
        g_ref[r * N_KEYS:(r + 1) * N_KEYS, ln] = acc

    @pl.when(j == 0)
    def _():
        y_ref[...] = jnp.zeros(y_ref.shape, F32)
        for r, c in pieces:
            gate_piece(ga_ref, 0, r, c)

    def step(g_cur, g_next):
        a_hi = jnp.minimum(j + 1, N_KEYS // PEER_ROWS - 1)
        todo = list(pieces)

        def gates(count):
            for _ in range(count):
                r, c = todo.pop(0)
                gate_piece(g_next, a_hi, r, c)

        n_split = 8
        per_dot = len(pieces) // (2 * n_split)
        kc = d_model // n_split
        h_t = None
        for q in range(n_split):
            part = lax.dot_general(u_ref[:, q * kc:(q + 1) * kc], xb_ref[:, q * kc:(q + 1) * kc],
                                   NT_DIMS, preferred_element_type=F32)
            h_t = part if h_t is None else part + h_t
            gates(per_dot)
        h_ref[...] = h_t
        for r, c in pieces:
            rows = slice(r * N_KEYS, (r + 1) * N_KEYS)
            ln = slice(c * LANES, (c + 1) * LANES)
            hh = h_ref[rows, ln]
            gelu = 0.5 * hh * (1.0 + lax.erf(hh * (2.0 ** -0.5)))
            a_ref[rows, ln] = g_cur[rows, ln] * gelu.astype(BF16)
        dc = d_model // n_split
        for q in range(n_split):
            cols = slice(q * dc, (q + 1) * dc)
            y_ref[:, cols] += lax.dot_general(a_ref[...], v_ref[:, cols], TN_DIMS,
                                              preferred_element_type=F32)
            gates(per_dot)
        gates(len(todo))

    @pl.when(j % 2 == 0)
    def _():
        step(ga_ref, gb_ref)

    @pl.when(j % 2 == 1)
    def _():
        step(gb_ref, ga_ref)


def _peer_experts(xb, route, u, v, tm):
    L, D = xb.shape
    th, c1, s2, e2 = route
    n_exp = u.shape[0]
    assert (n_exp // PEER_TE) % 2 == 0 and PEER_ROWS == 8
    big_spec = pl.BlockSpec((PEER_HEADS, N_KEYS, tm), lambda i, j: (0, 0, i))
    rows_spec = pl.BlockSpec((PEER_HEADS, N_KEYS // 8, 8, tm), lambda i, j: (0, 0, 0, i))
    return pl.pallas_call(
        _peer_kernel,
        grid=(L // tm, n_exp // PEER_TE),
        in_specs=[pl.BlockSpec((tm, D), lambda i, j: (i, 0)),
                  rows_spec, rows_spec, big_spec, big_spec,
                  pl.BlockSpec((PEER_TE, D), lambda i, j: (j, 0)),
                  pl.BlockSpec((PEER_TE, D), lambda i, j: (j, 0))],
        out_specs=pl.BlockSpec((tm, D), lambda i, j: (i, 0)),
        out_shape=jax.ShapeDtypeStruct((L, D), F32),
        scratch_shapes=[pltpu.VMEM((PEER_TE, tm), BF16), pltpu.VMEM((PEER_TE, tm), F32),
                        pltpu.VMEM((PEER_TE, tm), BF16), pltpu.VMEM((PEER_TE, tm), BF16)],
        compiler_params=_cparams("parallel", "arbitrary"),
        name="peer_experts",
    )(xb, th, c1, s2, e2, u, v)


def _cast_kernel(w_ref, o_ref):
    o_ref[...] = w_ref[...].astype(o_ref.dtype)


def _layer_to_bf16(w, layer, tr):
    _, R, C = w.shape
    return pl.pallas_call(
        _cast_kernel,
        grid=(R // tr,),
        in_specs=[pl.BlockSpec((None, tr, C), lambda r: (layer, r, 0))],
        out_specs=pl.BlockSpec((tr, C), lambda r: (r, 0)),
        out_shape=jax.ShapeDtypeStruct((R, C), BF16),
        compiler_params=_cparams("parallel"),
        name="cast_bf16",
    )(w)


def _add_ln_kernel(x_ref, y_ref, g_ref, b_ref, o_ref, ob_ref):
    out = _layer_norm(ALPHA * x_ref[...] + y_ref[...], g_ref[...], b_ref[...])
    o_ref[...] = out
    ob_ref[...] = out.astype(BF16)


def _add_ln(x, y, g, b, tm):
    L, D = x.shape
    row = lambda i: (i, 0)
    fixed = lambda i: (0, 0)
    return pl.pallas_call(
        _add_ln_kernel,
        grid=(L // tm,),
        in_specs=[pl.BlockSpec((tm, D), row), pl.BlockSpec((tm, D), row),
                  pl.BlockSpec((1, D), fixed), pl.BlockSpec((1, D), fixed)],
        out_specs=(pl.BlockSpec((tm, D), row), pl.BlockSpec((tm, D), row)),
        out_shape=(jax.ShapeDtypeStruct((L, D), F32), jax.ShapeDtypeStruct((L, D), BF16)),
        compiler_params=_cparams("parallel"),
        name="residual_ln",
    )(x, y, g.reshape(1, D), b.reshape(1, D))


def _fold(t, d):
    L, W = t.shape
    return t.reshape(L // d, d, W).transpose(1, 0, 2)


def _unfold(t):
    d, Lf, W = t.shape
    return t.transpose(1, 0, 2).reshape(Lf * d, W)


def _prep_in_weights(w_in):
    cols = lambda s: w_in[:, IN_OFFSETS[s]:IN_OFFSETS[s + 1]]
    w_rope = jnp.concatenate([cols(s) for s in ROPE_SLOTS], 1).astype(BF16)
    pad = jnp.zeros((w_in.shape[0], LANES - IN_SPLITS[SL_DV]), w_in.dtype)
    w_val = jnp.concatenate([cols(SL_BV), cols(SL_CV), cols(SL_AV), cols(SL_DV), pad], 1).astype(BF16)
    pad = jnp.zeros((w_in.shape[0], LANES - IN_SPLITS[SL_IW]), w_in.dtype)
    w_iw = jnp.concatenate([cols(SL_IW), pad], 1).astype(BF16)
    return w_rope, w_val, w_iw


def _mixer_layer(x, xin, tables, w_in, sinks, lam_p, sub_g, w_out, ln_g, ln_b, layer_idx, cfg):
    L = x.shape[0]
    w_rope, w_val, w_iw = _prep_in_weights(w_in)
    rp = _project(xin, w_rope, tables, BF16, cfg["proj_tm"], cfg["proj_tn"])
    vp = _project(xin, w_val, None, BF16, cfg["proj_tm"], VAL_WIDTH // 2)
    iw = _project(xin, w_iw, None, F32, cfg["proj_tm"], LANES)

    oa = _banded(rp[None], ROPE_OFF[SL_AQ] // 512, rp[None], ROPE_OFF[SL_AK] // LANES,
                 vp[None], VAL_OFF[SL_AV] // LANES, A_HEADS, A_KV_HEADS, SWA_WINDOW - 1,
                 sinks=sinks)[0]

    lam_init = 0.8 - 0.6 * math.exp(-0.3 * layer_idx)
    ob = _diff_attention(rp, vp, lam_p, sub_g, lam_init, cfg["diff_tq"], cfg["diff_tk"])

    cq = rp[:, ROPE_OFF[SL_CQ]:ROPE_OFF[SL_CQ] + 512]
    ck = rp[:, ROPE_OFF[SL_CK]:ROPE_OFF[SL_CK] + 512]
    cv = vp[:, VAL_OFF[SL_CV]:VAL_OFF[SL_CV] + 512]
    ocs, lses = [], []
    for window, dil in DILATION_PAIRS:
        o_c, lse_c = _banded(_fold(cq, dil), 0, _fold(ck, dil), 0, _fold(cv, dil), 0,
                             C_HEADS, C_HEADS, window // dil)
        ocs.append(_unfold(o_c))
        lses.append(_unfold(lse_c))

    dk = rp[:, ROPE_OFF[SL_DK]:ROPE_OFF[SL_DK] + HEAD_DIM]
    ik = rp[:, ROPE_OFF[SL_IK]:ROPE_OFF[SL_IK] + HEAD_DIM]
    dv = vp[:, VAL_OFF[SL_DV]:VAL_OFF[SL_DV] + LANES]
    dv = jnp.where(jnp.arange(LANES)[None, :] == HEAD_DIM, jnp.ones((), BF16), dv)
    od = _dsa_attention(rp, iw, dk, ik, dv, cfg["dsa_tk"])

    return _out_projection(x, oa, ob, ocs, lses, od, w_out.astype(BF16), ln_g, ln_b,
                           cfg["out_tm"])


def _peer_layer(x, xb, wq, keys, u_all, v_all, layer, ln_g, ln_b, cfg):
    route = _peer_route(xb, wq.astype(BF16),
                        keys.reshape(2 * PEER_HEADS, N_KEYS, PEER_HALF).astype(BF16),
                        cfg["route_tm"])
    u = _layer_to_bf16(u_all, layer, cfg["cast_tr"])
    v = _layer_to_bf16(v_all, layer, cfg["cast_tr"])
    y = _peer_experts(xb, route, u, v, cfg["peer_tm"])
    return _add_ln(x, y, ln_g, ln_b, cfg["out_tm"])


def _config(L):
    return dict(proj_tm=min(512, L), proj_tn=768, diff_tq=min(512, L), diff_tk=min(1024, L), dsa_tk=min(1024, L),
                out_tm=min(256, L), route_tm=min(256, L), peer_tm=min(512, L), cast_tr=512)


def _trunk(x2, w_in, attn_sinks, diff_lambda, diff_norm_g, w_out, ln1_g, ln1_b,
           peer_wq, peer_keys, peer_u, peer_v, ln2_g, ln2_b):
    L = x2.shape[0]
    cfg = _config(L)
    tables = _rope_tables(L, cfg["proj_tn"])
    xin = x2
    for i in range(w_in.shape[0]):
        x2, xb = _mixer_layer(x2, xin, tables, w_in[i], attn_sinks[i], diff_lambda[i],
                              diff_norm_g[i], w_out[i], ln1_g[i], ln1_b[i], i, cfg)
        x2, xb = _peer_layer(x2, xb, peer_wq[i], peer_keys[i], peer_u, peer_v, i,
                             ln2_g[i], ln2_b[i], cfg)
        xin = xb
    return x2


def kernel(x, w_in, attn_sinks, diff_lambda, diff_norm_g, w_out, ln1_g, ln1_b,
           peer_wq, peer_keys, peer_u, peer_v, ln2_g, ln2_b):
    b, L, D = x.shape
    outs = [_trunk(x.reshape(L, D) if b == 1 else x[i], w_in, attn_sinks, diff_lambda,
                   diff_norm_g, w_out, ln1_g, ln1_b, peer_wq, peer_keys, peer_u, peer_v, ln2_g,
                   ln2_b) for i in range(b)]
    return outs[0].reshape(1, L, D) if b == 1 else jnp.stack(outs, 0)
```

```python
import functools
import math

import jax
import jax.numpy as jnp
import numpy as np
from jax import lax
from jax.experimental import pallas as pl
from jax.experimental.pallas import tpu as pltpu

F32 = jnp.float32
BF16 = jnp.bfloat16
I32 = jnp.int32

D_MODEL = 2048
DEPTH = 2
HEAD_DIM = 64
HALF_DIM = HEAD_DIM // 2
LANES = 128
BLOCK = 128
NEG_INF = -1e30
ROPE_THETA = 10000.0
LN_EPS = 1e-5
RMS_EPS = 1e-5
A_HEADS, A_KV_HEADS, SWA_WINDOW = 8, 2, 128
B_HEADS = 4
C_HEADS = 8
DILATION_PAIRS = ((128, 1), (512, 4), (2048, 16))
D_HEADS, IDX_HEADS, IDX_DIM = 8, 8, 64
DSA_TOPK_MAX = 256
PEER_HEADS, PEER_TOPK, N_KEYS = 8, 16, 128
PEER_QDIM = 256
PEER_HALF = PEER_QDIM // 2
ALPHA = (2 * DEPTH) ** 0.25
Q_SCALE = HEAD_DIM ** -0.5
INT_MIN = -2 ** 31

IN_SPLITS = (512, 128, 128, 512, 512, 512, 512, 512, 512, 512, 64, 64, 512, 64, 8)
IN_OFFSETS = tuple(int(v) for v in np.concatenate([[0], np.cumsum(IN_SPLITS)]))
(SL_AQ, SL_AK, SL_AV, SL_BQ, SL_BK, SL_BV, SL_CQ, SL_CK, SL_CV,
 SL_DQ, SL_DK, SL_DV, SL_IQ, SL_IK, SL_IW) = range(15)
ROPE_SLOTS = (SL_AQ, SL_BQ, SL_BK, SL_CQ, SL_CK, SL_DQ, SL_IQ, SL_AK, SL_DK, SL_IK)
ROPE_OFF = {}
_o = 0
for _s in ROPE_SLOTS:
    ROPE_OFF[_s] = _o
    _o += IN_SPLITS[_s]
ROPE_WIDTH = _o
VAL_OFF = {SL_BV: 0, SL_CV: 512, SL_AV: 1024, SL_DV: 1152}
VAL_WIDTH = 1280
VMEM_LIMIT = 56 * 1024 * 1024

NT_DIMS = (((1,), (1,)), ((), ()))
TN_DIMS = (((0,), (0,)), ((), ()))


def _cparams(*sem):
    return pltpu.CompilerParams(dimension_semantics=sem, vmem_limit_bytes=VMEM_LIMIT)


def _proj_kernel(x_ref, w_ref, *rest, rope):
    if rope:
        cos_ref, sa_ref, sb_ref, o_ref, xb_ref = rest
    else:
        o_ref, xb_ref = rest

    @pl.when(pl.program_id(1) == 0)
    def _():
        xb_ref[...] = x_ref[...].astype(BF16)

    y = jnp.dot(xb_ref[...], w_ref[...], preferred_element_type=F32)
    if rope:
        for t in range(y.shape[1] // LANES):
            sl = slice(t * LANES, (t + 1) * LANES)
            yt = y[:, sl]
            o_ref[:, sl] = (yt * cos_ref[:, sl]
                            + pltpu.roll(yt, LANES - HALF_DIM, 1) * sa_ref[:, sl]
                            + pltpu.roll(yt, HALF_DIM, 1) * sb_ref[:, sl]).astype(o_ref.dtype)
    else:
        o_ref[...] = y.astype(o_ref.dtype)


def _project(x, w, tables, out_dtype, tm, tn):
    L, K = x.shape
    N = w.shape[1]
    rope = tables is not None
    in_specs = [pl.BlockSpec((tm, K), lambda i, j: (i, 0)),
                pl.BlockSpec((K, tn), lambda i, j: (0, j))]
    args = [x, w]
    if rope:
        in_specs += [pl.BlockSpec((tm, tn), lambda i, j: (i, 0))] * 3
        args += list(tables)
    return pl.pallas_call(
        functools.partial(_proj_kernel, rope=rope),
        grid=(L // tm, N // tn),
        in_specs=in_specs,
        out_specs=pl.BlockSpec((tm, tn), lambda i, j: (i, j)),
        out_shape=jax.ShapeDtypeStruct((L, N), out_dtype),
        scratch_shapes=[pltpu.VMEM((tm, K), BF16)],
        compiler_params=_cparams("parallel", "arbitrary"),
        name="proj_rope" if rope else "proj_plain",
    )(*args)


def _rope_tables(L, width):
    inv_freq = ROPE_THETA ** (-jnp.arange(HALF_DIM, dtype=F32) / HALF_DIM)
    ang = jnp.arange(L, dtype=F32)[:, None] * inv_freq[None, :]
    cos, sin = jnp.cos(ang), jnp.sin(ang)
    zero = jnp.zeros_like(sin)
    reps = width // HEAD_DIM
    cos_t = jnp.tile(jnp.concatenate([cos, cos], 1), (1, reps))
    sin_a = jnp.tile(jnp.concatenate([-sin, zero], 1), (1, reps))
    sin_b = jnp.tile(jnp.concatenate([zero, sin], 1), (1, reps))
    return cos_t, sin_a, sin_b


def _banded_kernel(*refs, n_kv, group, max_dist, use_sink):
    if use_sink:
        sink_ref, q_ref, kp_ref, kc_ref, vp_ref, vc_ref, o_ref = refs
    else:
        q_ref, kp_ref, kc_ref, vp_ref, vc_ref, o_ref = refs
    n = pl.program_id(1)
    q = q_ref[...] * Q_SCALE
    k2 = jnp.concatenate([kp_ref[...], kc_ref[...]], 0)
    v2 = jnp.concatenate([vp_ref[...], vc_ref[...]], 0)
    row = lax.broadcasted_iota(I32, (BLOCK, 2 * BLOCK), 0)
    col = lax.broadcasted_iota(I32, (BLOCK, 2 * BLOCK), 1)
    dist = row + BLOCK - col
    lo = jnp.where(n > 0, 0, BLOCK)
    ok = (dist >= 0) & (dist <= max_dist) & (col >= lo)
    lane = lax.broadcasted_iota(I32, (BLOCK, LANES), 1)
    n_heads = n_kv * group
    kh = [k2[:, g * HEAD_DIM:(g + 1) * HEAD_DIM] for g in range(n_kv)]
    vh = [v2[:, g * HEAD_DIM:(g + 1) * HEAD_DIM] for g in range(n_kv)]

    def scores(h):
        qh = q[:, h * HEAD_DIM:(h + 1) * HEAD_DIM]
        return lax.dot_general(qh, kh[h // group], NT_DIMS, preferred_element_type=F32)

    outs = []
    lse_tile = jnp.zeros((BLOCK, LANES), F32)
    s_next = scores(0)
    for h in range(n_heads):
        s = jnp.where(ok, s_next, NEG_INF)
        if h + 1 < n_heads:
            s_next = scores(h + 1)
        m = jnp.max(s, axis=1, keepdims=True)
        if use_sink:
            m = jnp.maximum(m, sink_ref[h])
        p = jnp.exp(s - m)
        l = jnp.sum(p, axis=1, keepdims=True)
        if use_sink:
            l = l + jnp.exp(sink_ref[h] - m)
        acc = jnp.dot(p.astype(BF16), vh[h // group], preferred_element_type=F32)
        outs.append(acc / l)
        if not use_sink:
            lse_tile = jnp.where(lane == h, m + jnp.log(l), lse_tile)
    if use_sink:
        o_ref[...] = jnp.concatenate(outs, 1).astype(o_ref.dtype)
    else:
        o_ref[...] = jnp.concatenate(outs + [lse_tile], 1)


def _banded(q3, qcol, k3, kcol, v3, vcol, n_heads, n_kv, max_dist, sinks=None):
    d, Lf, _ = q3.shape
    nb = Lf // BLOCK
    wq, wk = n_heads * HEAD_DIM, n_kv * HEAD_DIM
    use_sink = sinks is not None
    cur = lambda c: (lambda r, n: (r, n, c))
    prev = lambda c: (lambda r, n: (r, jnp.maximum(n - 1, 0), c))
    in_specs = [pl.BlockSpec((None, BLOCK, wq), cur(qcol)),
                pl.BlockSpec((None, BLOCK, wk), prev(kcol)),
                pl.BlockSpec((None, BLOCK, wk), cur(kcol)),
                pl.BlockSpec((None, BLOCK, wk), prev(vcol)),
                pl.BlockSpec((None, BLOCK, wk), cur(vcol))]
    args = [q3, k3, k3, v3, v3]
    if use_sink:
        in_specs = [pl.BlockSpec(memory_space=pltpu.SMEM)] + in_specs
        args = [sinks] + args
        out_shape = jax.ShapeDtypeStruct((d, Lf, wq), BF16)
        out_specs = pl.BlockSpec((None, BLOCK, wq), cur(0))
    else:
        out_shape = jax.ShapeDtypeStruct((d, Lf, wq + LANES), F32)
        out_specs = pl.BlockSpec((None, BLOCK, wq + LANES), cur(0))
    return pl.pallas_call(
        functools.partial(_banded_kernel, n_kv=n_kv, group=n_heads // n_kv,
                          max_dist=max_dist, use_sink=use_sink),
        grid=(d, nb),
        in_specs=in_specs,
        out_specs=out_specs,
        out_shape=out_shape,
        compiler_params=_cparams("parallel", "parallel"),
        name="swa_sink" if use_sink else "dilated_band",
    )(*args)


def _diff_kernel(lam_ref, g_ref, q_ref, k_ref, v_ref, o_ref, m_ref, acc_ref, *, tq, tk,
                 lam_init):
    i = pl.program_id(1)
    lane = lax.broadcasted_iota(I32, (tq, LANES), 1)
    q = q_ref[...] * Q_SCALE
    zero = jnp.zeros_like(q)
    qz = (jnp.where(lane < HEAD_DIM, q, zero), jnp.where(lane >= HEAD_DIM, q, zero))
    hq = tq // 2
    chains = [(s, r) for s in range(2) for r in range(2)]
    qc = [qz[s][r * hq:(r + 1) * hq] for s, r in chains]
    m_ref[...] = jnp.full(m_ref.shape, NEG_INF, F32)
    acc_ref[...] = jnp.zeros(acc_ref.shape, F32)
    ones_col = jnp.where(lax.broadcasted_iota(I32, (tk, LANES), 1) == 0, 1.0, 0.0).astype(BF16)

    def tile(j, masked):
        start = pl.multiple_of(j * tk, tk)
        kt = k_ref[pl.ds(start, tk), :]
        v_ext = jnp.concatenate([v_ref[pl.ds(start, tk), :], ones_col], 1)

        def scores(c):
            s = lax.dot_general(qc[c], kt, NT_DIMS, preferred_element_type=F32)
            if masked:
                qpos = i * tq + chains[c][1] * hq + lax.broadcasted_iota(I32, (hq, tk), 0)
                kpos = start + lax.broadcasted_iota(I32, (hq, tk), 1)
                s = jnp.where(kpos <= qpos, s, NEG_INF)
            return s

        s_next = scores(0)
        for c in range(len(chains)):
            s = s_next
            if c + 1 < len(chains):
                s_next = scores(c + 1)
            m_old = m_ref[c]
            m_new = jnp.maximum(m_old, jnp.max(s, axis=1, keepdims=True))
            p = jnp.exp(s - m_new).astype(BF16)
            acc_ref[c] = (jnp.exp(m_old - m_new) * acc_ref[c]
                          + jnp.dot(p, v_ext, preferred_element_type=F32))
            m_ref[c] = m_new

    def body(j, carry):
        tile(j, False)
        return carry

    n_full = (i * tq) // tk
    lax.fori_loop(0, n_full, body, 0)
    tile(n_full, True)

    lp = lam_ref[...]
    lam = (jnp.exp(jnp.sum(lp[0:1] * lp[1:2], axis=1, keepdims=True))
           - jnp.exp(jnp.sum(lp[2:3] * lp[3:4], axis=1, keepdims=True)) + lam_init)
    o = [acc_ref[c][:, :LANES] / acc_ref[c][:, LANES:LANES + 1] for c in range(len(chains))]
    ob = jnp.concatenate(o[0:2], 0) - lam * jnp.concatenate(o[2:4], 0)
    ob = ob * lax.rsqrt(jnp.mean(ob * ob, axis=1, keepdims=True) + RMS_EPS)
    o_ref[...] = (ob * g_ref[...] * (1.0 - lam_init)).astype(o_ref.dtype)


def _diff_attention(rp, vp, lam_p, sub_g, lam_init, tq, tk):
    L = rp.shape[0]
    qcol = ROPE_OFF[SL_BQ] // LANES
    kcol = ROPE_OFF[SL_BK] // LANES
    vcol = VAL_OFF[SL_BV] // LANES
    return pl.pallas_call(
        functools.partial(_diff_kernel, tq=tq, tk=tk, lam_init=lam_init),
        grid=(B_HEADS, L // tq),
        in_specs=[pl.BlockSpec((4, HEAD_DIM), lambda h, i: (0, 0)),
                  pl.BlockSpec((1, LANES), lambda h, i: (0, 0)),
                  pl.BlockSpec((tq, LANES), lambda h, i: (i, qcol + h)),
                  pl.BlockSpec((L, LANES), lambda h, i: (0, kcol + h)),
                  pl.BlockSpec((L, LANES), lambda h, i: (0, vcol + h))],
        out_specs=pl.BlockSpec((tq, LANES), lambda h, i: (i, h)),
        out_shape=jax.ShapeDtypeStruct((L, B_HEADS * LANES), BF16),
        scratch_shapes=[pltpu.VMEM((4, tq // 2, 1), F32),
                        pltpu.VMEM((4, tq // 2, 2 * LANES), F32)],
        compiler_params=_cparams("parallel", "arbitrary"),
        name="diff_attn",
    )(lam_p, sub_g.reshape(1, LANES), rp, rp, vp)


def _sortable(score):
    b = pltpu.bitcast(score, I32)
    return jnp.where(b >= 0, b, b ^ 0x7FFFFFFF)


DSA_PER = 2
DSA_CHAINS = D_HEADS // DSA_PER


def _dsa_kernel(q_ref, iq_ref, iw_ref, k_ref, ik_ref, v_ref, o_ref, key_ref, m_ref, acc_ref, *,
                tk, topk):
    n = pl.program_id(0)
    n_chunks = (n * BLOCK + BLOCK + tk - 1) // tk
    kpos = lax.broadcasted_iota(I32, (tk, BLOCK), 0)
    qpos = n * BLOCK + lax.broadcasted_iota(I32, (tk, BLOCK), 1)

    def chain_rows(t, g):
        return jnp.concatenate([t[:, h * HEAD_DIM:(h + 1) * HEAD_DIM]
                                for h in range(g * DSA_PER, (g + 1) * DSA_PER)], 0)

    def pipelined(first, rest):
        nxt = first(0)
        for g in range(DSA_CHAINS):
            cur = nxt
            if g + 1 < DSA_CHAINS:
                nxt = first(g + 1)
            rest(g, cur)

    iq_all = iq_ref[...] * (IDX_DIM ** -0.5)
    iqc = [chain_rows(iq_all, g) for g in range(DSA_CHAINS)]
    w_rows = (iw_ref[...] * (IDX_HEADS ** -0.5)).T

    def score_chunk(c, carry):
        start = pl.multiple_of(c * tk, tk)
        ikt = ik_ref[pl.ds(start, tk), :]
        parts = []

        def weighted(g, sc):
            part = None
            for u in range(DSA_PER):
                h = g * DSA_PER + u
                term = w_rows[h:h + 1] * jnp.maximum(sc[:, u * BLOCK:(u + 1) * BLOCK], 0.0)
                part = term if part is None else part + term
            parts.append(part)

        pipelined(lambda g: lax.dot_general(ikt, iqc[g], NT_DIMS, preferred_element_type=F32),
                  weighted)
        score = (parts[0] + parts[1]) + (parts[2] + parts[3])
        score = jnp.where(start + kpos <= qpos, score, NEG_INF)
        key_ref[c] = _sortable(score)
        return carry

    lax.fori_loop(0, n_chunks, score_chunk, 0)

    thr = jnp.full((1, BLOCK), INT_MIN, I32)
    for bit in range(31, -1, -1):
        cand = thr + np.int32(INT_MIN if bit == 31 else 1 << bit)

        def count_chunk(c, cnt, cand=cand):
            hit = jnp.where(key_ref[c] >= cand, 1, 0)
            return cnt + jnp.sum(hit.reshape(tk // 64, 64, BLOCK), axis=0)

        cnt = lax.fori_loop(0, n_chunks, count_chunk, jnp.zeros((64, BLOCK), I32))
        thr = jnp.where(jnp.sum(cnt, axis=0, keepdims=True) >= topk, cand, thr)

    q_all = q_ref[...] * Q_SCALE
    qc = [chain_rows(q_all, g) for g in range(DSA_CHAINS)]
    m_ref[...] = jnp.full(m_ref.shape, NEG_INF, F32)
    acc_ref[...] = jnp.zeros(acc_ref.shape, F32)

    def attend_chunk(c, carry):
        start = pl.multiple_of(c * tk, tk)
        keep = (key_ref[c] >= thr) & (start + kpos <= qpos)
        bias = jnp.where(keep, 0.0, NEG_INF).T[None]
        kt = k_ref[pl.ds(start, tk), :]
        vt = v_ref[pl.ds(start, tk), :]

        def softmax_pv(g, s):
            s = s.reshape(DSA_PER, BLOCK, tk) + bias
            m_old = m_ref[g]
            m_new = jnp.maximum(m_old, jnp.max(s, axis=2, keepdims=True))
            p = jnp.exp(s - jnp.maximum(m_new, 0.1 * NEG_INF))
            pv = jnp.dot(p.reshape(DSA_PER * BLOCK, tk).astype(BF16), vt,
                         preferred_element_type=F32)
            acc_ref[g] = (jnp.exp(m_old - m_new) * acc_ref[g]
                          + pv.reshape(DSA_PER, BLOCK, LANES))
            m_ref[g] = m_new

        pipelined(lambda g: lax.dot_general(qc[g], kt, NT_DIMS, preferred_element_type=F32),
                  softmax_pv)
        return carry

    lax.fori_loop(0, n_chunks, attend_chunk, 0)
    outs = []
    for g in range(DSA_CHAINS):
        acc = acc_ref[g]
        out = acc[:, :, :HEAD_DIM] / acc[:, :, HEAD_DIM:HEAD_DIM + 1]
        outs += [out[u] for u in range(DSA_PER)]
    o_ref[...] = jnp.concatenate(outs, 1).astype(o_ref.dtype)


def _dsa_attention(rp, iw, dk, ik, dv, tk):
    L = rp.shape[0]
    topk = min(DSA_TOPK_MAX, L // 4)
    wide = D_HEADS * HEAD_DIM
    return pl.pallas_call(
        functools.partial(_dsa_kernel, tk=tk, topk=topk),
        grid=(L // BLOCK,),
        in_specs=[pl.BlockSpec((BLOCK, wide), lambda n: (n, ROPE_OFF[SL_DQ] // wide)),
                  pl.BlockSpec((BLOCK, wide), lambda n: (n, ROPE_OFF[SL_IQ] // wide)),
                  pl.BlockSpec((BLOCK, LANES), lambda n: (n, 0)),
                  pl.BlockSpec((L, HEAD_DIM), lambda n: (0, 0)),
                  pl.BlockSpec((L, HEAD_DIM), lambda n: (0, 0)),
                  pl.BlockSpec((L, LANES), lambda n: (0, 0))],
        out_specs=pl.BlockSpec((BLOCK, wide), lambda n: (n, 0)),
        out_shape=jax.ShapeDtypeStruct((L, wide), BF16),
        scratch_shapes=[pltpu.VMEM((L // tk, tk, BLOCK), I32),
                        pltpu.VMEM((DSA_CHAINS, DSA_PER, BLOCK, 1), F32),
                        pltpu.VMEM((DSA_CHAINS, DSA_PER, BLOCK, LANES), F32)],
        compiler_params=_cparams("parallel"),
        name="dsa_attn",
    )(rp, rp, iw, dk, ik, dv)


def _layer_norm(z, g, b):
    mu = jnp.mean(z, axis=1, keepdims=True)
    zc = z - mu
    var = jnp.mean(zc * zc, axis=1, keepdims=True)
    return zc * lax.rsqrt(var + LN_EPS) * g + b


def _outproj_kernel(x_ref, oa_ref, ob_ref, oc1_ref, oc2_ref, oc3_ref, od_ref, w_ref, g_ref, b_ref,
                    y_ref, yb_ref):
    gw = C_HEADS * HEAD_DIM
    ocs = tuple(r[:, :gw] for r in (oc1_ref, oc2_ref, oc3_ref))
    lses = tuple(r[:, gw:] for r in (oc1_ref, oc2_ref, oc3_ref))
    top = jnp.maximum(jnp.maximum(lses[0], lses[1]), lses[2])
    wts = [jnp.exp(t - top) for t in lses]
    inv = 1.0 / (wts[0] + wts[1] + wts[2])
    merged = []
    for h in range(C_HEADS):
        sl = slice(h * HEAD_DIM, (h + 1) * HEAD_DIM)
        num = sum(wts[c][:, h:h + 1] * ocs[c][:, sl] for c in range(3))
        merged.append(num * inv[:, h:h + 1])
    oc = jnp.concatenate(merged, 1).astype(BF16)
    mixed = jnp.concatenate([oa_ref[...], ob_ref[...], oc, od_ref[...]], 1)
    y = jnp.dot(mixed, w_ref[...], preferred_element_type=F32)
    out = _layer_norm(ALPHA * x_ref[...] + y, g_ref[...], b_ref[...])
    y_ref[...] = out
    yb_ref[...] = out.astype(BF16)


def _out_projection(x, oa, ob, ocs, od, w, g, b, tm):
    L, D = x.shape
    gw = D // 4
    row = lambda i: (i, 0)
    fixed = lambda i: (0, 0)
    return pl.pallas_call(
        _outproj_kernel,
        grid=(L // tm,),
        in_specs=[pl.BlockSpec((tm, D), row), pl.BlockSpec((tm, gw), row),
                  pl.BlockSpec((tm, gw), row)]
                 + [pl.BlockSpec((tm, gw + LANES), row)] * 3
                 + [pl.BlockSpec((tm, gw), row), pl.BlockSpec((D, D), fixed),
                    pl.BlockSpec((1, D), fixed), pl.BlockSpec((1, D), fixed)],
        out_specs=(pl.BlockSpec((tm, D), row), pl.BlockSpec((tm, D), row)),
        out_shape=(jax.ShapeDtypeStruct((L, D), F32), jax.ShapeDtypeStruct((L, D), BF16)),
        compiler_params=_cparams("parallel"),
        name="outproj_ln",
    )(x, oa, ob, *ocs, od, w, g.reshape(1, D), b.reshape(1, D))


def _top_values(t, k, with_rank=False):
    vals = []
    rank = jnp.full(t.shape, float(k), F32)
    for i in range(k):
        mx = jnp.max(t, axis=0, keepdims=True)
        vals.append(mx)
        hit = t == mx
        if with_rank:
            rank = jnp.where(hit, float(i), rank)
        t = jnp.where(hit, -jnp.inf, t)
    return (vals, rank) if with_rank else vals


def _route_kernel(xb_ref, wq_ref, keys_ref, k1_ref, c1_ref, r2_ref, e2_ref):
    tm = xb_ref.shape[0]
    q = jnp.dot(xb_ref[...], wq_ref[...], preferred_element_type=F32).astype(BF16)
    for h in range(PEER_HEADS):
        st = []
        for p in range(2):
            lo = (2 * h + p) * PEER_HALF
            st.append(lax.dot_general(keys_ref[2 * h + p], q[:, lo:lo + PEER_HALF], NT_DIMS,
                                      preferred_element_type=F32))
        v1 = _top_values(st[0], PEER_TOPK)
        v2, rank2 = _top_values(st[1], PEER_TOPK, with_rank=True)
        v2_all = jnp.concatenate(v2, 0)
        row8 = lax.broadcasted_iota(I32, (8, tm), 0)
        cand = [v1[0] + v2_all]
        for i in range(1, 8):
            cand.append(jnp.where(row8 < PEER_TOPK // (i + 1), v1[i] + v2_all[:8], -jnp.inf))
        cand.append(jnp.concatenate(v1[8:], 0) + v2[0])
        mu = _top_values(jnp.concatenate(cand, 0), PEER_TOPK)
        z = sum(jnp.exp(m - mu[0]) for m in mu)
        tau = mu[PEER_TOPK - 1]
        k1 = jnp.zeros(st[0].shape, F32)
        for j in range(PEER_TOPK):
            k1 = k1 + jnp.where(st[0] + v2[j] >= tau, 1.0, 0.0)
        k1_ref[h] = k1.reshape(N_KEYS // 8, 8, tm)
        c1_ref[h] = (jnp.exp(st[0] - v1[0]) / z).reshape(N_KEYS // 8, 8, tm)
        r2_ref[h] = rank2.astype(BF16)
        e2_ref[h] = jnp.exp(st[1] - v2[0]).astype(BF16)


def _peer_route(xb, wq, keys, tm):
    L, D = xb.shape
    big = jax.ShapeDtypeStruct((PEER_HEADS, N_KEYS, L), BF16)
    big_spec = pl.BlockSpec((PEER_HEADS, N_KEYS, tm), lambda i: (0, 0, i))
    rows = jax.ShapeDtypeStruct((PEER_HEADS, N_KEYS // 8, 8, L), F32)
    rows_spec = pl.BlockSpec((PEER_HEADS, N_KEYS // 8, 8, tm), lambda i: (0, 0, 0, i))
    return pl.pallas_call(
        _route_kernel,
        grid=(L // tm,),
        in_specs=[pl.BlockSpec((tm, D), lambda i: (i, 0)),
                  pl.BlockSpec((D, PEER_HEADS * PEER_QDIM), lambda i: (0, 0)),
                  pl.BlockSpec((2 * PEER_HEADS, N_KEYS, PEER_HALF), lambda i: (0, 0, 0))],
        out_specs=(rows_spec, rows_spec, big_spec, big_spec),
        out_shape=(rows, rows, big, big),
        compiler_params=_cparams("parallel"),
        name="peer_route",
    )(xb, wq, keys)


PEER_ROWS = 8
PEER_TE = PEER_ROWS * N_KEYS


def _peer_kernel(xb_ref, k1_ref, c1_ref, r2_ref, e2_ref, u_ref, v_ref, y_ref, a_ref, h_ref,
                 ga_ref, gb_ref):
    j = pl.program_id(1)
    tm = xb_ref.shape[0]
    d_model = v_ref.shape[1]
    n_lane_tiles = tm // LANES
    pieces = [(r, c) for r in range(PEER_ROWS) for c in range(n_lane_tiles)]
    zero = jnp.zeros((), BF16)

    def gate_piece(g_ref, a_hi, r, c):
        ln = slice(c * LANES, (c + 1) * LANES)
        acc = jnp.zeros((N_KEYS, LANES), BF16)
        for h in range(PEER_HEADS):
            k1 = k1_ref[h, a_hi, r:r + 1, ln].astype(BF16)
            c1 = c1_ref[h, a_hi, r:r + 1, ln].astype(BF16)
            acc = acc + jnp.where(r2_ref[h, :, ln] < k1, e2_ref[h, :, ln] * c1, zero)
        g_ref[r * N_KEYS:(r + 1) * N_KEYS, ln] = acc

    @pl.when(j == 0)
    def _():
        y_ref[...] = jnp.zeros(y_ref.shape, F32)
        for r, c in pieces:
            gate_piece(ga_ref, 0, r, c)

    def step(g_cur, g_next):
        a_hi = jnp.minimum(j + 1, N_KEYS // PEER_ROWS - 1)
        todo = list(pieces)

        def gates(count):
            for _ in range(count):
                r, c = todo.pop(0)
                gate_piece(g_next, a_hi, r, c)

        n_split = 8
        per_dot = len(pieces) // (2 * n_split)
        kc = d_model // n_split
        h_t = None
        for q in range(n_split):
            part = lax.dot_general(u_ref[:, q * kc:(q + 1) * kc], xb_ref[:, q * kc:(q + 1) * kc],
                                   NT_DIMS, preferred_element_type=F32)
            h_t = part if h_t is None else part + h_t
            gates(per_dot)
        h_ref[...] = h_t
        for r, c in pieces:
            rows = slice(r * N_KEYS, (r + 1) * N_KEYS)
            ln = slice(c * LANES, (c + 1) * LANES)
            hh = h_ref[rows, ln]
            gelu = 0.5 * hh * (1.0 + lax.erf(hh * (2.0 ** -0.5)))
            a_ref[rows, ln] = g_cur[rows, ln] * gelu.astype(BF16)
        dc = d_model // n_split
        for q in range(n_split):
            cols = slice(q * dc, (q + 1) * dc)
            y_ref[:, cols] += lax.dot_general(a_ref[...], v_ref[:, cols], TN_DIMS,
                                              preferred_element_type=F32)
            gates(per_dot)
        gates(len(todo))

    @pl.when(j % 2 == 0)
    def _():
        step(ga_ref, gb_ref)

    @pl.when(j % 2 == 1)
    def _():
        step(gb_ref, ga_ref)


def _peer_experts(xb, route, u, v, tm):
    L, D = xb.shape
    th, c1, s2, e2 = route
    n_exp = u.shape[0]
    assert (n_exp // PEER_TE) % 2 == 0 and PEER_ROWS == 8
    big_spec = pl.BlockSpec((PEER_HEADS, N_KEYS, tm), lambda i, j: (0, 0, i))
    rows_spec = pl.BlockSpec((PEER_HEADS, N_KEYS // 8, 8, tm), lambda i, j: (0, 0, 0, i))
    return pl.pallas_call(
        _peer_kernel,
        grid=(L // tm, n_exp // PEER_TE),
        in_specs=[pl.BlockSpec((tm, D), lambda i, j: (i, 0)),
                  rows_spec, rows_spec, big_spec, big_spec,
                  pl.BlockSpec((PEER_TE, D), lambda i, j: (j, 0)),
                  pl.BlockSpec((PEER_TE, D), lambda i, j: (j, 0))],
        out_specs=pl.BlockSpec((tm, D), lambda i, j: (i, 0)),
        out_shape=jax.ShapeDtypeStruct((L, D), F32),
        scratch_shapes=[pltpu.VMEM((PEER_TE, tm), BF16), pltpu.VMEM((PEER_TE, tm), F32),
                        pltpu.VMEM((PEER_TE, tm), BF16), pltpu.VMEM((PEER_TE, tm), BF16)],
        compiler_params=_cparams("parallel", "arbitrary"),
        name="peer_experts",
    )(xb, th, c1, s2, e2, u, v)


def _cast_kernel(w_ref, o_ref):
    o_ref[...] = w_ref[...].astype(o_ref.dtype)


def _layer_to_bf16(w, layer, tr):
    _, R, C = w.shape
    return pl.pallas_call(
        _cast_kernel,
        grid=(R // tr,),
        in_specs=[pl.BlockSpec((None, tr, C), lambda r: (layer, r, 0))],
        out_specs=pl.BlockSpec((tr, C), lambda r: (r, 0)),
        out_shape=jax.ShapeDtypeStruct((R, C), BF16),
        compiler_params=_cparams("parallel"),
        name="cast_bf16",
    )(w)


def _add_ln_kernel(x_ref, y_ref, g_ref, b_ref, o_ref, ob_ref):
    out = _layer_norm(ALPHA * x_ref[...] + y_ref[...], g_ref[...], b_ref[...])
    o_ref[...] = out
    ob_ref[...] = out.astype(BF16)


def _add_ln(x, y, g, b, tm):
    L, D = x.shape
    row = lambda i: (i, 0)
    fixed = lambda i: (0, 0)
    return pl.pallas_call(
        _add_ln_kernel,
        grid=(L // tm,),
        in_specs=[pl.BlockSpec((tm, D), row), pl.BlockSpec((tm, D), row),
                  pl.BlockSpec((1, D), fixed), pl.BlockSpec((1, D), fixed)],
        out_specs=(pl.BlockSpec((tm, D), row), pl.BlockSpec((tm, D), row)),
        out_shape=(jax.ShapeDtypeStruct((L, D), F32), jax.ShapeDtypeStruct((L, D), BF16)),
        compiler_params=_cparams("parallel"),
        name="residual_ln",
    )(x, y, g.reshape(1, D), b.reshape(1, D))


def _fold(t, d):
    L, W = t.shape
    return t.reshape(L // d, d, W).transpose(1, 0, 2)


def _unfold(t):
    d, Lf, W = t.shape
    return t.transpose(1, 0, 2).reshape(Lf * d, W)


def _prep_in_weights(w_in):
    cols = lambda s: w_in[:, IN_OFFSETS[s]:IN_OFFSETS[s + 1]]
    w_rope = jnp.concatenate([cols(s) for s in ROPE_SLOTS], 1).astype(BF16)
    pad = jnp.zeros((w_in.shape[0], LANES - IN_SPLITS[SL_DV]), w_in.dtype)
    w_val = jnp.concatenate([cols(SL_BV), cols(SL_CV), cols(SL_AV), cols(SL_DV), pad], 1).astype(BF16)
    pad = jnp.zeros((w_in.shape[0], LANES - IN_SPLITS[SL_IW]), w_in.dtype)
    w_iw = jnp.concatenate([cols(SL_IW), pad], 1).astype(BF16)
    return w_rope, w_val, w_iw


def _mixer_layer(x, xin, tables, w_in, sinks, lam_p, sub_g, w_out, ln_g, ln_b, layer_idx, cfg):
    L = x.shape[0]
    w_rope, w_val, w_iw = _prep_in_weights(w_in)
    rp = _project(xin, w_rope, tables, BF16, cfg["proj_tm"], cfg["proj_tn"])
    vp = _project(xin, w_val, None, BF16, cfg["proj_tm"], VAL_WIDTH // 2)
    iw = _project(xin, w_iw, None, F32, cfg["proj_tm"], LANES)

    oa = _banded(rp[None], ROPE_OFF[SL_AQ] // 512, rp[None], ROPE_OFF[SL_AK] // LANES,
                 vp[None], VAL_OFF[SL_AV] // LANES, A_HEADS, A_KV_HEADS, SWA_WINDOW - 1,
                 sinks=sinks)[0]

    lam_init = 0.8 - 0.6 * math.exp(-0.3 * layer_idx)
    ob = _diff_attention(rp, vp, lam_p, sub_g, lam_init, cfg["diff_tq"], cfg["diff_tk"])

    cqkv = jnp.concatenate([rp[:, ROPE_OFF[SL_CQ]:ROPE_OFF[SL_CK] + 512],
                            vp[:, VAL_OFF[SL_CV]:VAL_OFF[SL_CV] + 512]], 1)
    ocs = []
    for window, dil in DILATION_PAIRS:
        folded = _fold(cqkv, dil)
        ocs.append(_unfold(_banded(folded, 0, folded, 1, folded, 2, C_HEADS, C_HEADS,
                                   window // dil)))

    dk = rp[:, ROPE_OFF[SL_DK]:ROPE_OFF[SL_DK] + HEAD_DIM]
    ik = rp[:, ROPE_OFF[SL_IK]:ROPE_OFF[SL_IK] + HEAD_DIM]
    dv = vp[:, VAL_OFF[SL_DV]:VAL_OFF[SL_DV] + LANES]
    dv = jnp.where(jnp.arange(LANES)[None, :] == HEAD_DIM, jnp.ones((), BF16), dv)
    od = _dsa_attention(rp, iw, dk, ik, dv, cfg["dsa_tk"])

    return _out_projection(x, oa, ob, ocs, od, w_out.astype(BF16), ln_g, ln_b,
                           cfg["out_tm"])


def _peer_layer(x, xb, wq, keys, u_all, v_all, layer, ln_g, ln_b, cfg):
    route = _peer_route(xb, wq.astype(BF16),
                        keys.reshape(2 * PEER_HEADS, N_KEYS, PEER_HALF).astype(BF16),
                        cfg["route_tm"])
    u = _layer_to_bf16(u_all, layer, cfg["cast_tr"])
    v = _layer_to_bf16(v_all, layer, cfg["cast_tr"])
    y = _peer_experts(xb, route, u, v, cfg["peer_tm"])
    return _add_ln(x, y, ln_g, ln_b, cfg["out_tm"])


def _config(L):
    return dict(proj_tm=min(512, L), proj_tn=768, diff_tq=min(512, L), diff_tk=min(1024, L), dsa_tk=min(1024, L),
                out_tm=min(256, L), route_tm=min(256, L), peer_tm=min(512, L), cast_tr=512)


def _trunk(x2, w_in, attn_sinks, diff_lambda, diff_norm_g, w_out, ln1_g, ln1_b,
           peer_wq, peer_keys, peer_u, peer_v, ln2_g, ln2_b):
    L = x2.shape[0]
    cfg = _config(L)
    tables = _rope_tables(L, cfg["proj_tn"])
    xin = x2
    for i in range(w_in.shape[0]):
        x2, xb = _mixer_layer(x2, xin, tables, w_in[i], attn_sinks[i], diff_lambda[i],
                              diff_norm_g[i], w_out[i], ln1_g[i], ln1_b[i], i, cfg)
        x2, xb = _peer_layer(x2, xb, peer_wq[i], peer_keys[i], peer_u, peer_v, i,
                             ln2_g[i], ln2_b[i], cfg)
        xin = xb
    return x2


def kernel(x, w_in, attn_sinks, diff_lambda, diff_norm_g, w_out, ln1_g, ln1_b,
           peer_wq, peer_keys, peer_u, peer_v, ln2_g, ln2_b):
    b, L, D = x.shape
    outs = [_trunk(x.reshape(L, D) if b == 1 else x[i], w_in, attn_sinks, diff_lambda,
                   diff_norm_g, w_out, ln1_g, ln1_b, peer_wq, peer_keys, peer_u, peer_v, ln2_g,
                   ln2_b) for i in range(b)]
    return outs[0].reshape(1, L, D) if b == 1 else jnp.stack(outs, 0)
```

```python
import functools
import math

import jax
import jax.numpy as jnp
import numpy as np
from jax import lax
from jax.experimental import pallas as pl
from jax.experimental.pallas import tpu as pltpu

F32 = jnp.float32
BF16 = jnp.bfloat16
I32 = jnp.int32

D_MODEL = 2048
DEPTH = 2
HEAD_DIM = 64
HALF_DIM = HEAD_DIM // 2
LANES = 128
BLOCK = 128
NEG_INF = -1e30
ROPE_THETA = 10000.0
LN_EPS = 1e-5
RMS_EPS = 1e-5
A_HEADS, A_KV_HEADS, SWA_WINDOW = 8, 2, 128
B_HEADS = 4
C_HEADS = 8
DILATION_PAIRS = ((128, 1), (512, 4), (2048, 16))
D_HEADS, IDX_HEADS, IDX_DIM = 8, 8, 64
DSA_TOPK_MAX = 256
PEER_HEADS, PEER_TOPK, N_KEYS = 8, 16, 128
PEER_QDIM = 256
PEER_HALF = PEER_QDIM // 2
ALPHA = (2 * DEPTH) ** 0.25
Q_SCALE = HEAD_DIM ** -0.5
INT_MIN = -2 ** 31

IN_SPLITS = (512, 128, 128, 512, 512, 512, 512, 512, 512, 512, 64, 64, 512, 64, 8)
IN_OFFSETS = tuple(int(v) for v in np.concatenate([[0], np.cumsum(IN_SPLITS)]))
(SL_AQ, SL_AK, SL_AV, SL_BQ, SL_BK, SL_BV, SL_CQ, SL_CK, SL_CV,
 SL_DQ, SL_DK, SL_DV, SL_IQ, SL_IK, SL_IW) = range(15)
ROPE_SLOTS = (SL_AQ, SL_BQ, SL_BK, SL_CQ, SL_CK, SL_DQ, SL_IQ, SL_AK, SL_DK, SL_IK)
ROPE_OFF = {}
_o = 0
for _s in ROPE_SLOTS:
    ROPE_OFF[_s] = _o
    _o += IN_SPLITS[_s]
ROPE_WIDTH = _o
VAL_OFF = {SL_BV: 0, SL_CV: 512, SL_AV: 1024, SL_DV: 1152}
VAL_WIDTH = 1280
VMEM_LIMIT = 56 * 1024 * 1024

NT_DIMS = (((1,), (1,)), ((), ()))
TN_DIMS = (((0,), (0,)), ((), ()))


def _cparams(*sem):
    return pltpu.CompilerParams(dimension_semantics=sem, vmem_limit_bytes=VMEM_LIMIT)


def _proj_kernel(x_ref, w_ref, *rest, rope):
    if rope:
        cos_ref, sa_ref, sb_ref, o_ref, xb_ref = rest
    else:
        o_ref, xb_ref = rest

    @pl.when(pl.program_id(1) == 0)
    def _():
        xb_ref[...] = x_ref[...].astype(BF16)

    y = jnp.dot(xb_ref[...], w_ref[...], preferred_element_type=F32)
    if rope:
        for t in range(y.shape[1] // LANES):
            sl = slice(t * LANES, (t + 1) * LANES)
            yt = y[:, sl]
            o_ref[:, sl] = (yt * cos_ref[:, sl]
                            + pltpu.roll(yt, LANES - HALF_DIM, 1) * sa_ref[:, sl]
                            + pltpu.roll(yt, HALF_DIM, 1) * sb_ref[:, sl]).astype(o_ref.dtype)
    else:
        o_ref[...] = y.astype(o_ref.dtype)


def _project(x, w, tables, out_dtype, tm, tn):
    L, K = x.shape
    N = w.shape[1]
    rope = tables is not None
    in_specs = [pl.BlockSpec((tm, K), lambda i, j: (i, 0)),
                pl.BlockSpec((K, tn), lambda i, j: (0, j))]
    args = [x, w]
    if rope:
        in_specs += [pl.BlockSpec((tm, tn), lambda i, j: (i, 0))] * 3
        args += list(tables)
    return pl.pallas_call(
        functools.partial(_proj_kernel, rope=rope),
        grid=(L // tm, N // tn),
        in_specs=in_specs,
        out_specs=pl.BlockSpec((tm, tn), lambda i, j: (i, j)),
        out_shape=jax.ShapeDtypeStruct((L, N), out_dtype),
        scratch_shapes=[pltpu.VMEM((tm, K), BF16)],
        compiler_params=_cparams("parallel", "arbitrary"),
        name="proj_rope" if rope else "proj_plain",
    )(*args)


def _rope_tables(L, width):
    inv_freq = ROPE_THETA ** (-jnp.arange(HALF_DIM, dtype=F32) / HALF_DIM)
    ang = jnp.arange(L, dtype=F32)[:, None] * inv_freq[None, :]
    cos, sin = jnp.cos(ang), jnp.sin(ang)
    zero = jnp.zeros_like(sin)
    reps = width // HEAD_DIM
    cos_t = jnp.tile(jnp.concatenate([cos, cos], 1), (1, reps))
    sin_a = jnp.tile(jnp.concatenate([-sin, zero], 1), (1, reps))
    sin_b = jnp.tile(jnp.concatenate([zero, sin], 1), (1, reps))
    return cos_t, sin_a, sin_b


def _banded_kernel(*refs, n_kv, group, max_dist, use_sink):
    if use_sink:
        sink_ref, q_ref, kp_ref, kc_ref, vp_ref, vc_ref, o_ref = refs
    else:
        q_ref, kp_ref, kc_ref, vp_ref, vc_ref, o_ref = refs
    n = pl.program_id(1)
    q = q_ref[...] * Q_SCALE
    k2 = jnp.concatenate([kp_ref[...], kc_ref[...]], 0)
    v2 = jnp.concatenate([vp_ref[...], vc_ref[...]], 0)
    row = lax.broadcasted_iota(I32, (BLOCK, 2 * BLOCK), 0)
    col = lax.broadcasted_iota(I32, (BLOCK, 2 * BLOCK), 1)
    dist = row + BLOCK - col
    lo = jnp.where(n > 0, 0, BLOCK)
    ok = (dist >= 0) & (dist <= max_dist) & (col >= lo)
    lane = lax.broadcasted_iota(I32, (BLOCK, LANES), 1)
    n_heads = n_kv * group
    kh = [k2[:, g * HEAD_DIM:(g + 1) * HEAD_DIM] for g in range(n_kv)]
    vh = [v2[:, g * HEAD_DIM:(g + 1) * HEAD_DIM] for g in range(n_kv)]

    def scores(h):
        qh = q[:, h * HEAD_DIM:(h + 1) * HEAD_DIM]
        return lax.dot_general(qh, kh[h // group], NT_DIMS, preferred_element_type=F32)

    outs = []
    lse_tile = jnp.zeros((BLOCK, LANES), F32)
    s_next = scores(0)
    for h in range(n_heads):
        s = jnp.where(ok, s_next, NEG_INF)
        if h + 1 < n_heads:
            s_next = scores(h + 1)
        m = jnp.max(s, axis=1, keepdims=True)
        if use_sink:
            m = jnp.maximum(m, sink_ref[h])
        p = jnp.exp(s - m)
        l = jnp.sum(p, axis=1, keepdims=True)
        if use_sink:
            l = l + jnp.exp(sink_ref[h] - m)
        acc = jnp.dot(p.astype(BF16), vh[h // group], preferred_element_type=F32)
        outs.append(acc / l)
        if not use_sink:
            lse_tile = jnp.where(lane == h, m + jnp.log(l), lse_tile)
    if use_sink:
        o_ref[...] = jnp.concatenate(outs, 1).astype(o_ref.dtype)
    else:
        o_ref[...] = jnp.concatenate(outs + [lse_tile], 1)


def _banded(q3, qcol, k3, kcol, v3, vcol, n_heads, n_kv, max_dist, sinks=None):
    d, Lf, _ = q3.shape
    nb = Lf // BLOCK
    wq, wk = n_heads * HEAD_DIM, n_kv * HEAD_DIM
    use_sink = sinks is not None
    cur = lambda c: (lambda r, n: (r, n, c))
    prev = lambda c: (lambda r, n: (r, jnp.maximum(n - 1, 0), c))
    in_specs = [pl.BlockSpec((None, BLOCK, wq), cur(qcol)),
                pl.BlockSpec((None, BLOCK, wk), prev(kcol)),
                pl.BlockSpec((None, BLOCK, wk), cur(kcol)),
                pl.BlockSpec((None, BLOCK, wk), prev(vcol)),
                pl.BlockSpec((None, BLOCK, wk), cur(vcol))]
    args = [q3, k3, k3, v3, v3]
    if use_sink:
        in_specs = [pl.BlockSpec(memory_space=pltpu.SMEM)] + in_specs
        args = [sinks] + args
        out_shape = jax.ShapeDtypeStruct((d, Lf, wq), BF16)
        out_specs = pl.BlockSpec((None, BLOCK, wq), cur(0))
    else:
        out_shape = jax.ShapeDtypeStruct((d, Lf, wq + LANES), F32)
        out_specs = pl.BlockSpec((None, BLOCK, wq + LANES), cur(0))
    return pl.pallas_call(
        functools.partial(_banded_kernel, n_kv=n_kv, group=n_heads // n_kv,
                          max_dist=max_dist, use_sink=use_sink),
        grid=(d, nb),
        in_specs=in_specs,
        out_specs=out_specs,
        out_shape=out_shape,
        compiler_params=_cparams("parallel", "parallel"),
        name="swa_sink" if use_sink else "dilated_band",
    )(*args)


def _diff_kernel(lam_ref, g_ref, q_ref, k_ref, v_ref, o_ref, m_ref, acc_ref, *, tq, tk,
                 lam_init):
    i = pl.program_id(1)
    lane = lax.broadcasted_iota(I32, (tq, LANES), 1)
    q = q_ref[...] * Q_SCALE
    zero = jnp.zeros_like(q)
    qz = (jnp.where(lane < HEAD_DIM, q, zero), jnp.where(lane >= HEAD_DIM, q, zero))
    hq = tq // 2
    chains = [(s, r) for s in range(2) for r in range(2)]
    qc = [qz[s][r * hq:(r + 1) * hq] for s, r in chains]
    m_ref[...] = jnp.full(m_ref.shape, NEG_INF, F32)
    acc_ref[...] = jnp.zeros(acc_ref.shape, F32)
    ones_col = jnp.where(lax.broadcasted_iota(I32, (tk, LANES), 1) == 0, 1.0, 0.0).astype(BF16)

    def tile(j, masked):
        start = pl.multiple_of(j * tk, tk)
        kt = k_ref[pl.ds(start, tk), :]
        v_ext = jnp.concatenate([v_ref[pl.ds(start, tk), :], ones_col], 1)

        def scores(c):
            s = lax.dot_general(qc[c], kt, NT_DIMS, preferred_element_type=F32)
            if masked:
                qpos = i * tq + chains[c][1] * hq + lax.broadcasted_iota(I32, (hq, tk), 0)
                kpos = start + lax.broadcasted_iota(I32, (hq, tk), 1)
                s = jnp.where(kpos <= qpos, s, NEG_INF)
            return s

        s_next = scores(0)
        for c in range(len(chains)):
            s = s_next
            if c + 1 < len(chains):
                s_next = scores(c + 1)
            m_old = m_ref[c]
            m_new = jnp.maximum(m_old, jnp.max(s, axis=1, keepdims=True))
            p = jnp.exp(s - m_new).astype(BF16)
            acc_ref[c] = (jnp.exp(m_old - m_new) * acc_ref[c]
                          + jnp.dot(p, v_ext, preferred_element_type=F32))
            m_ref[c] = m_new

    def body(j, carry):
        tile(j, False)
        return carry

    n_full = (i * tq) // tk
    lax.fori_loop(0, n_full, body, 0)
    tile(n_full, True)

    lp = lam_ref[...]
    lam = (jnp.exp(jnp.sum(lp[0:1] * lp[1:2], axis=1, keepdims=True))
           - jnp.exp(jnp.sum(lp[2:3] * lp[3:4], axis=1, keepdims=True)) + lam_init)
    o = [acc_ref[c][:, :LANES] / acc_ref[c][:, LANES:LANES + 1] for c in range(len(chains))]
    ob = jnp.concatenate(o[0:2], 0) - lam * jnp.concatenate(o[2:4], 0)
    ob = ob * lax.rsqrt(jnp.mean(ob * ob, axis=1, keepdims=True) + RMS_EPS)
    o_ref[...] = (ob * g_ref[...] * (1.0 - lam_init)).astype(o_ref.dtype)


def _diff_attention(rp, vp, lam_p, sub_g, lam_init, tq, tk):
    L = rp.shape[0]
    qcol = ROPE_OFF[SL_BQ] // LANES
    kcol = ROPE_OFF[SL_BK] // LANES
    vcol = VAL_OFF[SL_BV] // LANES
    return pl.pallas_call(
        functools.partial(_diff_kernel, tq=tq, tk=tk, lam_init=lam_init),
        grid=(B_HEADS, L // tq),
        in_specs=[pl.BlockSpec((4, HEAD_DIM), lambda h, i: (0, 0)),
                  pl.BlockSpec((1, LANES), lambda h, i: (0, 0)),
                  pl.BlockSpec((tq, LANES), lambda h, i: (i, qcol + h)),
                  pl.BlockSpec((L, LANES), lambda h, i: (0, kcol + h)),
                  pl.BlockSpec((L, LANES), lambda h, i: (0, vcol + h))],
        out_specs=pl.BlockSpec((tq, LANES), lambda h, i: (i, h)),
        out_shape=jax.ShapeDtypeStruct((L, B_HEADS * LANES), BF16),
        scratch_shapes=[pltpu.VMEM((4, tq // 2, 1), F32),
                        pltpu.VMEM((4, tq // 2, 2 * LANES), F32)],
        compiler_params=_cparams("parallel", "arbitrary"),
        name="diff_attn",
    )(lam_p, sub_g.reshape(1, LANES), rp, rp, vp)


def _sortable(score):
    b = pltpu.bitcast(score, I32)
    return jnp.where(b >= 0, b, b ^ 0x7FFFFFFF)


DSA_PER = 2
DSA_CHAINS = D_HEADS // DSA_PER


def _dsa_kernel(q_ref, iq_ref, iw_ref, k_ref, ik_ref, v_ref, o_ref, key_ref, m_ref, acc_ref, *,
                tk, topk):
    n = pl.program_id(0)
    n_chunks = (n * BLOCK + BLOCK + tk - 1) // tk
    kpos = lax.broadcasted_iota(I32, (tk, BLOCK), 0)
    qpos = n * BLOCK + lax.broadcasted_iota(I32, (tk, BLOCK), 1)

    def chain_rows(t, g):
        return jnp.concatenate([t[:, h * HEAD_DIM:(h + 1) * HEAD_DIM]
                                for h in range(g * DSA_PER, (g + 1) * DSA_PER)], 0)

    def pipelined(first, rest):
        nxt = first(0)
        for g in range(DSA_CHAINS):
            cur = nxt
            if g + 1 < DSA_CHAINS:
                nxt = first(g + 1)
            rest(g, cur)

    iq_all = iq_ref[...] * (IDX_DIM ** -0.5)
    iqc = [chain_rows(iq_all, g) for g in range(DSA_CHAINS)]
    w_rows = (iw_ref[...] * (IDX_HEADS ** -0.5)).T

    def score_chunk(c, carry):
        start = pl.multiple_of(c * tk, tk)
        ikt = ik_ref[pl.ds(start, tk), :]
        parts = []

        def weighted(g, sc):
            part = None
            for u in range(DSA_PER):
                h = g * DSA_PER + u
                term = w_rows[h:h + 1] * jnp.maximum(sc[:, u * BLOCK:(u + 1) * BLOCK], 0.0)
                part = term if part is None else part + term
            parts.append(part)

        pipelined(lambda g: lax.dot_general(ikt, iqc[g], NT_DIMS, preferred_element_type=F32),
                  weighted)
        score = functools.reduce(lambda a, b: a + b, parts)
        score = jnp.where(start + kpos <= qpos, score, NEG_INF)
        key_ref[c] = _sortable(score)
        return carry

    lax.fori_loop(0, n_chunks, score_chunk, 0)

    thr = jnp.full((1, BLOCK), INT_MIN, I32)
    for bit in range(31, -1, -1):
        cand = thr + np.int32(INT_MIN if bit == 31 else 1 << bit)

        def count_chunk(c, cnt, cand=cand):
            hit = jnp.where(key_ref[c] >= cand, 1, 0)
            return cnt + jnp.sum(hit.reshape(tk // 64, 64, BLOCK), axis=0)

        cnt = lax.fori_loop(0, n_chunks, count_chunk, jnp.zeros((64, BLOCK), I32))
        thr = jnp.where(jnp.sum(cnt, axis=0, keepdims=True) >= topk, cand, thr)

    q_all = q_ref[...] * Q_SCALE
    qc = [chain_rows(q_all, g) for g in range(DSA_CHAINS)]
    m_ref[...] = jnp.full(m_ref.shape, NEG_INF, F32)
    acc_ref[...] = jnp.zeros(acc_ref.shape, F32)

    def attend_chunk(c, carry):
        start = pl.multiple_of(c * tk, tk)
        keep = (key_ref[c] >= thr) & (start + kpos <= qpos)
        bias = jnp.where(keep, 0.0, NEG_INF).T[None]
        kt = k_ref[pl.ds(start, tk), :]
        vt = v_ref[pl.ds(start, tk), :]

        def softmax_pv(g, s):
            s = s.reshape(DSA_PER, BLOCK, tk) + bias
            m_old = m_ref[g]
            m_new = jnp.maximum(m_old, jnp.max(s, axis=2, keepdims=True))
            p = jnp.exp(s - jnp.maximum(m_new, 0.1 * NEG_INF))
            pv = jnp.dot(p.reshape(DSA_PER * BLOCK, tk).astype(BF16), vt,
                         preferred_element_type=F32)
            acc_ref[g] = (jnp.exp(m_old - m_new) * acc_ref[g]
                          + pv.reshape(DSA_PER, BLOCK, LANES))
            m_ref[g] = m_new

        pipelined(lambda g: lax.dot_general(qc[g], kt, NT_DIMS, preferred_element_type=F32),
                  softmax_pv)
        return carry

    lax.fori_loop(0, n_chunks, attend_chunk, 0)
    outs = []
    for g in range(DSA_CHAINS):
        acc = acc_ref[g]
        out = acc[:, :, :HEAD_DIM] / acc[:, :, HEAD_DIM:HEAD_DIM + 1]
        outs += [out[u] for u in range(DSA_PER)]
    o_ref[...] = jnp.concatenate(outs, 1).astype(o_ref.dtype)


def _dsa_attention(rp, iw, dk, ik, dv, tk):
    L = rp.shape[0]
    topk = min(DSA_TOPK_MAX, L // 4)
    wide = D_HEADS * HEAD_DIM
    return pl.pallas_call(
        functools.partial(_dsa_kernel, tk=tk, topk=topk),
        grid=(L // BLOCK,),
        in_specs=[pl.BlockSpec((BLOCK, wide), lambda n: (n, ROPE_OFF[SL_DQ] // wide)),
                  pl.BlockSpec((BLOCK, wide), lambda n: (n, ROPE_OFF[SL_IQ] // wide)),
                  pl.BlockSpec((BLOCK, LANES), lambda n: (n, 0)),
                  pl.BlockSpec((L, HEAD_DIM), lambda n: (0, 0)),
                  pl.BlockSpec((L, HEAD_DIM), lambda n: (0, 0)),
                  pl.BlockSpec((L, LANES), lambda n: (0, 0))],
        out_specs=pl.BlockSpec((BLOCK, wide), lambda n: (n, 0)),
        out_shape=jax.ShapeDtypeStruct((L, wide), BF16),
        scratch_shapes=[pltpu.VMEM((L // tk, tk, BLOCK), I32),
                        pltpu.VMEM((DSA_CHAINS, DSA_PER, BLOCK, 1), F32),
                        pltpu.VMEM((DSA_CHAINS, DSA_PER, BLOCK, LANES), F32)],
        compiler_params=_cparams("parallel"),
        name="dsa_attn",
    )(rp, rp, iw, dk, ik, dv)


def _layer_norm(z, g, b):
    mu = jnp.mean(z, axis=1, keepdims=True)
    zc = z - mu
    var = jnp.mean(zc * zc, axis=1, keepdims=True)
    return zc * lax.rsqrt(var + LN_EPS) * g + b


def _outproj_kernel(x_ref, oa_ref, ob_ref, oc1_ref, oc2_ref, oc3_ref, od_ref, w_ref, g_ref, b_ref,
                    y_ref, yb_ref):
    gw = C_HEADS * HEAD_DIM
    ocs = tuple(r[:, :gw] for r in (oc1_ref, oc2_ref, oc3_ref))
    lses = tuple(r[:, gw:] for r in (oc1_ref, oc2_ref, oc3_ref))
    y = (jnp.dot(oa_ref[...], w_ref[0:gw, :], preferred_element_type=F32)
         + jnp.dot(ob_ref[...], w_ref[gw:2 * gw, :], preferred_element_type=F32)
         + jnp.dot(od_ref[...], w_ref[3 * gw:4 * gw, :], preferred_element_type=F32))
    top = jnp.maximum(jnp.maximum(lses[0], lses[1]), lses[2])
    wts = [jnp.exp(t - top) for t in lses]
    inv = 1.0 / (wts[0] + wts[1] + wts[2])
    merged = []
    for h in range(C_HEADS):
        sl = slice(h * HEAD_DIM, (h + 1) * HEAD_DIM)
        num = sum(wts[c][:, h:h + 1] * ocs[c][:, sl] for c in range(3))
        merged.append(num * inv[:, h:h + 1])
    oc = jnp.concatenate(merged, 1).astype(BF16)
    y = jnp.dot(oc, w_ref[2 * gw:3 * gw, :], preferred_element_type=F32) + y
    out = _layer_norm(ALPHA * x_ref[...] + y, g_ref[...], b_ref[...])
    y_ref[...] = out
    yb_ref[...] = out.astype(BF16)


def _out_projection(x, oa, ob, ocs, od, w, g, b, tm):
    L, D = x.shape
    gw = D // 4
    row = lambda i: (i, 0)
    fixed = lambda i: (0, 0)
    return pl.pallas_call(
        _outproj_kernel,
        grid=(L // tm,),
        in_specs=[pl.BlockSpec((tm, D), row), pl.BlockSpec((tm, gw), row),
                  pl.BlockSpec((tm, gw), row)]
                 + [pl.BlockSpec((tm, gw + LANES), row)] * 3
                 + [pl.BlockSpec((tm, gw), row), pl.BlockSpec((D, D), fixed),
                    pl.BlockSpec((1, D), fixed), pl.BlockSpec((1, D), fixed)],
        out_specs=(pl.BlockSpec((tm, D), row), pl.BlockSpec((tm, D), row)),
        out_shape=(jax.ShapeDtypeStruct((L, D), F32), jax.ShapeDtypeStruct((L, D), BF16)),
        compiler_params=_cparams("parallel"),
        name="outproj_ln",
    )(x, oa, ob, *ocs, od, w, g.reshape(1, D), b.reshape(1, D))


def _top_values(t, k, with_rank=False):
    vals = []
    rank = jnp.full(t.shape, float(k), F32)
    for i in range(k):
        mx = jnp.max(t, axis=0, keepdims=True)
        vals.append(mx)
        hit = t == mx
        if with_rank:
            rank = jnp.where(hit, float(i), rank)
        t = jnp.where(hit, -jnp.inf, t)
    return (vals, rank) if with_rank else vals


def _route_kernel(xb_ref, wq_ref, keys_ref, k1_ref, c1_ref, r2_ref, e2_ref):
    tm = xb_ref.shape[0]
    q = jnp.dot(xb_ref[...], wq_ref[...], preferred_element_type=F32).astype(BF16)
    for h in range(PEER_HEADS):
        st = []
        for p in range(2):
            lo = (2 * h + p) * PEER_HALF
            st.append(lax.dot_general(keys_ref[2 * h + p], q[:, lo:lo + PEER_HALF], NT_DIMS,
                                      preferred_element_type=F32))
        v1 = _top_values(st[0], PEER_TOPK)
        v2, rank2 = _top_values(st[1], PEER_TOPK, with_rank=True)
        v2_all = jnp.concatenate(v2, 0)
        row8 = lax.broadcasted_iota(I32, (8, tm), 0)
        cand = [v1[0] + v2_all]
        for i in range(1, 8):
            cand.append(jnp.where(row8 < PEER_TOPK // (i + 1), v1[i] + v2_all[:8], -jnp.inf))
        cand.append(jnp.concatenate(v1[8:], 0) + v2[0])
        mu = _top_values(jnp.concatenate(cand, 0), PEER_TOPK)
        z = sum(jnp.exp(m - mu[0]) for m in mu)
        tau = mu[PEER_TOPK - 1]
        k1 = jnp.zeros(st[0].shape, F32)
        for j in range(PEER_TOPK):
            k1 = k1 + jnp.where(st[0] + v2[j] >= tau, 1.0, 0.0)
        k1_ref[h] = k1.reshape(N_KEYS // 8, 8, tm)
        c1_ref[h] = (jnp.exp(st[0] - v1[0]) / z).reshape(N_KEYS // 8, 8, tm)
        r2_ref[h] = rank2.astype(BF16)
        e2_ref[h] = jnp.exp(st[1] - v2[0]).astype(BF16)


def _peer_route(xb, wq, keys, tm):
    L, D = xb.shape
    big = jax.ShapeDtypeStruct((PEER_HEADS, N_KEYS, L), BF16)
    big_spec = pl.BlockSpec((PEER_HEADS, N_KEYS, tm), lambda i: (0, 0, i))
    rows = jax.ShapeDtypeStruct((PEER_HEADS, N_KEYS // 8, 8, L), F32)
    rows_spec = pl.BlockSpec((PEER_HEADS, N_KEYS // 8, 8, tm), lambda i: (0, 0, 0, i))
    return pl.pallas_call(
        _route_kernel,
        grid=(L // tm,),
        in_specs=[pl.BlockSpec((tm, D), lambda i: (i, 0)),
                  pl.BlockSpec((D, PEER_HEADS * PEER_QDIM), lambda i: (0, 0)),
                  pl.BlockSpec((2 * PEER_HEADS, N_KEYS, PEER_HALF), lambda i: (0, 0, 0))],
        out_specs=(rows_spec, rows_spec, big_spec, big_spec),
        out_shape=(rows, rows, big, big),
        compiler_params=_cparams("parallel"),
        name="peer_route",
    )(xb, wq, keys)


PEER_ROWS = 8
PEER_TE = PEER_ROWS * N_KEYS


def _peer_kernel(xb_ref, k1_ref, c1_ref, r2_ref, e2_ref, u_ref, v_ref, y_ref, a_ref, h_ref,
                 ga_ref, gb_ref):
    j = pl.program_id(1)
    tm = xb_ref.shape[0]
    d_model = v_ref.shape[1]
    n_lane_tiles = tm // LANES
    pieces = [(r, c) for r in range(PEER_ROWS) for c in range(n_lane_tiles)]
    zero = jnp.zeros((), BF16)

    def gate_piece(g_ref, a_hi, r, c):
        ln = slice(c * LANES, (c + 1) * LANES)
        acc = jnp.zeros((N_KEYS, LANES), BF16)
        for h in range(PEER_HEADS):
            k1 = k1_ref[h, a_hi, r:r + 1, ln].astype(BF16)
            c1 = c1_ref[h, a_hi, r:r + 1, ln].astype(BF16)
            acc = acc + jnp.where(r2_ref[h, :, ln] < k1, e2_ref[h, :, ln] * c1, zero)
        g_ref[r * N_KEYS:(r + 1) * N_KEYS, ln] = acc

    @pl.when(j == 0)
    def _():
        y_ref[...] = jnp.zeros(y_ref.shape, F32)
        for r, c in pieces:
            gate_piece(ga_ref, 0, r, c)

    def step(g_cur, g_next):
        a_hi = jnp.minimum(j + 1, N_KEYS // PEER_ROWS - 1)
        todo = list(pieces)

        def gates(count):
            for _ in range(count):
                r, c = todo.pop(0)
                gate_piece(g_next, a_hi, r, c)

        n_split = 8
        per_dot = len(pieces) // (2 * n_split)
        kc = d_model // n_split
        h_t = None
        for q in range(n_split):
            part = lax.dot_general(u_ref[:, q * kc:(q + 1) * kc], xb_ref[:, q * kc:(q + 1) * kc],
                                   NT_DIMS, preferred_element_type=F32)
            h_t = part if h_t is None else part + h_t
            gates(per_dot)
        h_ref[...] = h_t
        for r, c in pieces:
            rows = slice(r * N_KEYS, (r + 1) * N_KEYS)
            ln = slice(c * LANES, (c + 1) * LANES)
            hh = h_ref[rows, ln]
            gelu = 0.5 * hh * (1.0 + lax.erf(hh * (2.0 ** -0.5)))
            a_ref[rows, ln] = g_cur[rows, ln] * gelu.astype(BF16)
        dc = d_model // n_split
        for q in range(n_split):
            cols = slice(q * dc, (q + 1) * dc)
            y_ref[:, cols] += lax.dot_general(a_ref[...], v_ref[:, cols], TN_DIMS,
                                              preferred_element_type=F32)
            gates(per_dot)
        gates(len(todo))

    @pl.when(j % 2 == 0)
    def _():
        step(ga_ref, gb_ref)

    @pl.when(j % 2 == 1)
    def _():
        step(gb_ref, ga_ref)


def _peer_experts(xb, route, u, v, tm):
    L, D = xb.shape
    th, c1, s2, e2 = route
    n_exp = u.shape[0]
    assert (n_exp // PEER_TE) % 2 == 0 and PEER_ROWS == 8
    big_spec = pl.BlockSpec((PEER_HEADS, N_KEYS, tm), lambda i, j: (0, 0, i))
    rows_spec = pl.BlockSpec((PEER_HEADS, N_KEYS // 8, 8, tm), lambda i, j: (0, 0, 0, i))
    return pl.pallas_call(
        _peer_kernel,
        grid=(L // tm, n_exp // PEER_TE),
        in_specs=[pl.BlockSpec((tm, D), lambda i, j: (i, 0)),
                  rows_spec, rows_spec, big_spec, big_spec,
                  pl.BlockSpec((PEER_TE, D), lambda i, j: (j, 0)),
                  pl.BlockSpec((PEER_TE, D), lambda i, j: (j, 0))],
        out_specs=pl.BlockSpec((tm, D), lambda i, j: (i, 0)),
        out_shape=jax.ShapeDtypeStruct((L, D), F32),
        scratch_shapes=[pltpu.VMEM((PEER_TE, tm), BF16), pltpu.VMEM((PEER_TE, tm), F32),
                        pltpu.VMEM((PEER_TE, tm), BF16), pltpu.VMEM((PEER_TE, tm), BF16)],
        compiler_params=_cparams("parallel", "arbitrary"),
        name="peer_experts",
    )(xb, th, c1, s2, e2, u, v)


def _cast_kernel(w_ref, o_ref):
    o_ref[...] = w_ref[...].astype(o_ref.dtype)


def _layer_to_bf16(w, layer, tr):
    _, R, C = w.shape
    return pl.pallas_call(
        _cast_kernel,
        grid=(R // tr,),
        in_specs=[pl.BlockSpec((None, tr, C), lambda r: (layer, r, 0))],
        out_specs=pl.BlockSpec((tr, C), lambda r: (r, 0)),
        out_shape=jax.ShapeDtypeStruct((R, C), BF16),
        compiler_params=_cparams("parallel"),
        name="cast_bf16",
    )(w)


def _add_ln_kernel(x_ref, y_ref, g_ref, b_ref, o_ref, ob_ref):
    out = _layer_norm(ALPHA * x_ref[...] + y_ref[...], g_ref[...], b_ref[...])
    o_ref[...] = out
    ob_ref[...] = out.astype(BF16)


def _add_ln(x, y, g, b, tm):
    L, D = x.shape
    row = lambda i: (i, 0)
    fixed = lambda i: (0, 0)
    return pl.pallas_call(
        _add_ln_kernel,
        grid=(L // tm,),
        in_specs=[pl.BlockSpec((tm, D), row), pl.BlockSpec((tm, D), row),
                  pl.BlockSpec((1, D), fixed), pl.BlockSpec((1, D), fixed)],
        out_specs=(pl.BlockSpec((tm, D), row), pl.BlockSpec((tm, D), row)),
        out_shape=(jax.ShapeDtypeStruct((L, D), F32), jax.ShapeDtypeStruct((L, D), BF16)),
        compiler_params=_cparams("parallel"),
        name="residual_ln",
    )(x, y, g.reshape(1, D), b.reshape(1, D))


def _fold(t, d):
    L, W = t.shape
    return t.reshape(L // d, d, W).transpose(1, 0, 2)


def _unfold(t):
    d, Lf, W = t.shape
    return t.transpose(1, 0, 2).reshape(Lf * d, W)


def _prep_in_weights(w_in):
    cols = lambda s: w_in[:, IN_OFFSETS[s]:IN_OFFSETS[s + 1]]
    w_rope = jnp.concatenate([cols(s) for s in ROPE_SLOTS], 1).astype(BF16)
    pad = jnp.zeros((w_in.shape[0], LANES - IN_SPLITS[SL_DV]), w_in.dtype)
    w_val = jnp.concatenate([cols(SL_BV), cols(SL_CV), cols(SL_AV), cols(SL_DV), pad], 1).astype(BF16)
    pad = jnp.zeros((w_in.shape[0], LANES - IN_SPLITS[SL_IW]), w_in.dtype)
    w_iw = jnp.concatenate([cols(SL_IW), pad], 1).astype(BF16)
    return w_rope, w_val, w_iw


def _mixer_layer(x, xin, tables, w_in, sinks, lam_p, sub_g, w_out, ln_g, ln_b, layer_idx, cfg):
    L = x.shape[0]
    w_rope, w_val, w_iw = _prep_in_weights(w_in)
    rp = _project(xin, w_rope, tables, BF16, cfg["proj_tm"], cfg["proj_tn"])
    vp = _project(xin, w_val, None, BF16, cfg["proj_tm"], VAL_WIDTH // 2)
    iw = _project(xin, w_iw, None, F32, cfg["proj_tm"], LANES)

    oa = _banded(rp[None], ROPE_OFF[SL_AQ] // 512, rp[None], ROPE_OFF[SL_AK] // LANES,
                 vp[None], VAL_OFF[SL_AV] // LANES, A_HEADS, A_KV_HEADS, SWA_WINDOW - 1,
                 sinks=sinks)[0]

    lam_init = 0.8 - 0.6 * math.exp(-0.3 * layer_idx)
    ob = _diff_attention(rp, vp, lam_p, sub_g, lam_init, cfg["diff_tq"], cfg["diff_tk"])

    cqkv = jnp.concatenate([rp[:, ROPE_OFF[SL_CQ]:ROPE_OFF[SL_CK] + 512],
                            vp[:, VAL_OFF[SL_CV]:VAL_OFF[SL_CV] + 512]], 1)
    ocs = []
    for window, dil in DILATION_PAIRS:
        folded = _fold(cqkv, dil)
        ocs.append(_unfold(_banded(folded, 0, folded, 1, folded, 2, C_HEADS, C_HEADS,
                                   window // dil)))

    dk = rp[:, ROPE_OFF[SL_DK]:ROPE_OFF[SL_DK] + HEAD_DIM]
    ik = rp[:, ROPE_OFF[SL_IK]:ROPE_OFF[SL_IK] + HEAD_DIM]
    dv = vp[:, VAL_OFF[SL_DV]:VAL_OFF[SL_DV] + LANES]
    dv = jnp.where(jnp.arange(LANES)[None, :] == HEAD_DIM, jnp.ones((), BF16), dv)
    od = _dsa_attention(rp, iw, dk, ik, dv, cfg["dsa_tk"])

    return _out_projection(x, oa, ob, ocs, od, w_out.astype(BF16), ln_g, ln_b,
                           cfg["out_tm"])


def _peer_layer(x, xb, wq, keys, u_all, v_all, layer, ln_g, ln_b, cfg):
    route = _peer_route(xb, wq.astype(BF16),
                        keys.reshape(2 * PEER_HEADS, N_KEYS, PEER_HALF).astype(BF16),
                        cfg["route_tm"])
    u = _layer_to_bf16(u_all, layer, cfg["cast_tr"])
    v = _layer_to_bf16(v_all, layer, cfg["cast_tr"])
    y = _peer_experts(xb, route, u, v, cfg["peer_tm"])
    return _add_ln(x, y, ln_g, ln_b, cfg["out_tm"])


def _config(L):
    return dict(proj_tm=min(512, L), proj_tn=768, diff_tq=min(512, L), diff_tk=min(1024, L), dsa_tk=min(1024, L),
                out_tm=min(256, L), route_tm=min(256, L), peer_tm=min(512, L), cast_tr=512)


def _trunk(x2, w_in, attn_sinks, diff_lambda, diff_norm_g, w_out, ln1_g, ln1_b,
           peer_wq, peer_keys, peer_u, peer_v, ln2_g, ln2_b):
    L = x2.shape[0]
    cfg = _config(L)
    tables = _rope_tables(L, cfg["proj_tn"])
    xin = x2
    for i in range(w_in.shape[0]):
        x2, xb = _mixer_layer(x2, xin, tables, w_in[i], attn_sinks[i], diff_lambda[i],
                              diff_norm_g[i], w_out[i], ln1_g[i], ln1_b[i], i, cfg)
        x2, xb = _peer_layer(x2, xb, peer_wq[i], peer_keys[i], peer_u, peer_v, i,
                             ln2_g[i], ln2_b[i], cfg)
        xin = xb
    return x2


def kernel(x, w_in, attn_sinks, diff_lambda, diff_norm_g, w_out, ln1_g, ln1_b,
           peer_wq, peer_keys, peer_u, peer_v, ln2_g, ln2_b):
    b, L, D = x.shape
    outs = [_trunk(x.reshape(L, D) if b == 1 else x[i], w_in, attn_sinks, diff_lambda,
                   diff_norm_g, w_out, ln1_g, ln1_b, peer_wq, peer_keys, peer_u, peer_v, ln2_g,
                   ln2_b) for i in range(b)]
    return outs[0].reshape(1, L, D) if b == 1 else jnp.stack(outs, 0)
```
